```python
import jax, jax.numpy as jnp
from jax import lax
import numpy as np

D_MODEL = 1024
BATCH = 2
SEQ = 8192
DEPTH = 4
DEC_BATCH = 32
DEC_SEQ = 8
PAST_LEN = 8192
PAGE_SIZE = 128

NH_M = 4
DH_M = D_MODEL // 8
W_M = NH_M * DH_M
M_CHUNK = 128
CONV_CH = D_MODEL // 2
CONV_W = 31
NH_SB = 8
DH_SB = D_MODEL // 16
W_SB = NH_SB * DH_SB
Q_BLOCK = 128
SB_BIAS_INIT = -8.0
N_BRANCH = 3
N_EXP = 32
TOP_K = 4
D_FF = D_MODEL
SWIGLU_ALPHA = 1.702
SWIGLU_LIMIT = 7.0
EPS = 1e-5

OFF_MQ = 0
OFF_MK = OFF_MQ + W_M
OFF_MV = OFF_MK + W_M
OFF_MO = OFF_MV + W_M
OFF_MIF = OFF_MO + W_M
OFF_CG = OFF_MIF + 2 * NH_M
OFF_SQ = OFF_CG + 2 * CONV_CH
OFF_SK = OFF_SQ + W_SB
OFF_SV = OFF_SK + W_SB
OFF_G = OFF_SV + W_SB
D_IN = OFF_G + N_BRANCH * D_MODEL

kernel_name = 'hybrid_mlstm_conformer_stickbreak_moe_step'


def _rmsnorm(x, w):
    xf = x.astype(jnp.float32)
    y = xf * lax.rsqrt(jnp.mean(xf * xf, axis=-1, keepdims=True) + EPS)
    return (y * w.astype(jnp.float32)).astype(x.dtype)


def _layernorm(x, w, b):
    xf = x.astype(jnp.float32)
    mu = jnp.mean(xf, axis=-1, keepdims=True)
    var = jnp.mean(jnp.square(xf - mu), axis=-1, keepdims=True)
    y = (xf - mu) * lax.rsqrt(var + EPS) * w.astype(jnp.float32) + b.astype(jnp.float32)
    return y.astype(x.dtype)


def _headnorm(h, w):
    mu = jnp.mean(h, axis=-1, keepdims=True)
    var = jnp.mean(jnp.square(h - mu), axis=-1, keepdims=True)
    return (h - mu) * lax.rsqrt(var + EPS) * w.astype(jnp.float32)


def _adaln(c, w_ada, b_ada):
    mod = jax.nn.silu(c) @ w_ada + b_ada
    return jnp.split(mod[:, None, :], 6, axis=-1)


def _mlstm(q, k, v, i_pre, logf, C0, n0, m0):
    f32 = jnp.float32
    B, L, H, d = q.shape
    cs = M_CHUNK if L % M_CHUNK == 0 else L
    nc = L // cs

    def chunks(a):
        a = a.astype(f32)
        return jnp.moveaxis(a.reshape((B, nc, cs) + a.shape[2:]), 1, 0)

    causal = jnp.tril(jnp.ones((cs, cs), dtype=bool))

    def step(carry, xs):
        C, n, m = carry
        qb, kb, vb, ib, fb = xs
        F = jnp.cumsum(fb, axis=1).transpose(0, 2, 1)
        ig = ib.transpose(0, 2, 1)
        dlog = jnp.where(causal, F[..., :, None] - F[..., None, :] + ig[..., None, :], -jnp.inf)
        b_inter = F + m[..., None]
        m_t = jnp.maximum(b_inter, jnp.max(dlog, axis=-1))
        s = jnp.einsum('bthd,bshd->bhts', qb, kb) * jnp.exp(dlog - m_t[..., None])
        a_inter = jnp.exp(b_inter - m_t)
        num = (jnp.einsum('bhts,bshd->bthd', s, vb)
               + a_inter.transpose(0, 2, 1)[..., None] * jnp.einsum('bhvk,bthk->bthv', C, qb))
        den = jnp.sum(s, axis=-1) + a_inter * jnp.einsum('bhk,bthk->bht', n, qb)
        den = jnp.maximum(jnp.abs(den), jnp.exp(-m_t))
        h = num / den.transpose(0, 2, 1)[..., None]
        F_end = F[..., -1]
        dec = F_end[..., None] - F + ig
        m_new = jnp.maximum(F_end + m, jnp.max(dec, axis=-1))
        w_s = jnp.exp(dec - m_new[..., None])
        keep = jnp.exp(F_end + m - m_new)
        C_new = keep[..., None, None] * C + jnp.einsum('bhs,bshv,bshk->bhvk', w_s, vb, kb)
        n_new = keep[..., None] * n + jnp.einsum('bhs,bshk->bhk', w_s, kb)
        return (C_new, n_new, m_new), h

    xs = (chunks(q), chunks(k), chunks(v), chunks(i_pre), chunks(logf))
    (C, n, m), hs = lax.scan(step, (C0.astype(f32), n0.astype(f32), m0.astype(f32)), xs)
    h = jnp.moveaxis(hs, 0, 1).reshape(B, L, H, d)
    return h, C, n, m


def _causal_depthwise(a, buf, w, b):
    up = jnp.concatenate([buf.astype(a.dtype), a], axis=1)
    y = lax.conv_general_dilated(up, w[:, None, :].astype(a.dtype), window_strides=(1,), padding='VALID',
                                 dimension_numbers=('NWC', 'WIO', 'NWC'), feature_group_count=a.shape[-1])
    return y + b, up[:, -(CONV_W - 1):]


def _stick_breaking(q, k, v, q_offset, sb_b):
    f32 = jnp.float32
    B, Lq, H, d = q.shape
    Lk = k.shape[1]
    qb = Q_BLOCK if Lq % Q_BLOCK == 0 else Lq
    nb = Lq // qb
    qblocks = jnp.moveaxis(q.astype(f32).reshape(B, nb, qb, H, d), 1, 0)
    kf = k.astype(f32)
    vf = v.astype(f32)
    key_pos = jnp.arange(Lk)
    scale = d ** -0.5
    bias = sb_b.astype(f32)[None, :, None, None]

    def block(args):
        qblk, bi = args
        q_pos = q_offset + bi * qb + jnp.arange(qb)
        z = jnp.einsum('bqhd,bkhd->bhqk', qblk, kf) * scale + bias
        visible = key_pos[None, :] < q_pos[:, None]
        log_beta = jax.nn.log_sigmoid(z)
        log_1m = jnp.where(visible, jax.nn.log_sigmoid(-z), 0.0)
        later = lax.cumsum(log_1m, axis=3, reverse=True) - log_1m
        A = jnp.where(visible, jnp.exp(log_beta + later), 0.0)
        return jnp.einsum('bhqk,bkhd->bqhd', A, vf)

    out = lax.map(block, (qblocks, jnp.arange(nb)))
    return jnp.moveaxis(out, 0, 1).reshape(B, Lq, H, d).astype(q.dtype)


def _token_mixer(h, mC, mn, mm, conv_buf, k_past, v_past, q_offset,
                 w_in, gate_b, mnorm_w, w_pm, dw_w, dw_b, ln_w, ln_b, w_pc, w_pa, sb_b, w_out):
    B, L, _ = h.shape
    u = h @ w_in
    q_m = u[..., OFF_MQ:OFF_MK].reshape(B, L, NH_M, DH_M)
    k_m = u[..., OFF_MK:OFF_MV].reshape(B, L, NH_M, DH_M) * (DH_M ** -0.5)
    v_m = u[..., OFF_MV:OFF_MO].reshape(B, L, NH_M, DH_M)
    o_m = jax.nn.sigmoid(u[..., OFF_MO:OFF_MIF])
    if_pre = u[..., OFF_MIF:OFF_CG].astype(jnp.float32) + gate_b.astype(jnp.float32)
    i_pre = if_pre[..., :NH_M]
    logf = jax.nn.log_sigmoid(if_pre[..., NH_M:])
    h_m, mC, mn, mm = _mlstm(q_m, k_m, v_m, i_pre, logf, mC, mn, mm)
    h_m = _headnorm(h_m, mnorm_w.reshape(NH_M, DH_M)).reshape(B, L, W_M).astype(h.dtype) * o_m
    y_m = h_m @ w_pm
    g = u[..., OFF_CG:OFF_SQ]
    a = g[..., :CONV_CH] * jax.nn.sigmoid(g[..., CONV_CH:])
    a_conv, conv_buf = _causal_depthwise(a, conv_buf, dw_w, dw_b)
    y_c = jax.nn.silu(_layernorm(a_conv, ln_w, ln_b)) @ w_pc
    q_s = u[..., OFF_SQ:OFF_SK].reshape(B, L, NH_SB, DH_SB)
    k_s = u[..., OFF_SK:OFF_SV].reshape(B, L, NH_SB, DH_SB)
    v_s = u[..., OFF_SV:OFF_G].reshape(B, L, NH_SB, DH_SB)
    k_all = jnp.concatenate([k_past.astype(k_s.dtype), k_s], axis=1)
    v_all = jnp.concatenate([v_past.astype(v_s.dtype), v_s], axis=1)
    y_a = _stick_breaking(q_s, k_all, v_all, q_offset, sb_b).reshape(B, L, W_SB) @ w_pa
    gates = jax.nn.sigmoid(u[..., OFF_G:]).reshape(B, L, N_BRANCH, D_MODEL)
    merged = gates[..., 0, :] * y_m + gates[..., 1, :] * y_c + gates[..., 2, :] * y_a
    return merged @ w_out, k_s, v_s, mC, mn, mm, conv_buf


def _moe(h, router_w, router_b, w1, b1, w2, b2):
    f32 = jnp.float32
    N = h.shape[0]
    logits = (h @ router_w).astype(f32) + router_b.astype(f32)
    top_v, top_i = lax.top_k(logits, TOP_K)
    top_p = jax.nn.softmax(top_v, axis=-1)
    gates = jnp.einsum('nk,nke->en', top_p, jax.nn.one_hot(top_i, N_EXP, dtype=f32))

    def expert(acc, ew):
        w1e, b1e, w2e, b2e, ge = ew
        z = h @ w1e + b1e
        z_glu = jnp.minimum(z[:, :D_FF], SWIGLU_LIMIT)
        z_lin = jnp.clip(z[:, D_FF:], -SWIGLU_LIMIT, SWIGLU_LIMIT)
        act = z_glu * jax.nn.sigmoid(SWIGLU_ALPHA * z_glu) * (z_lin + 1)
        return acc + ge[:, None] * (act @ w2e + b2e), None

    acc, _ = lax.scan(expert, jnp.zeros((N, D_MODEL), f32), (w1, b1, w2, b2, gates))
    return acc.astype(h.dtype)


def _trunk(x, c, mC0, mn0, mm0, conv0, cache_k, cache_v, page_table, weights):
    (w_ada, b_ada, norm1_w, norm2_w, w_in, mlstm_gate_b, mlstm_norm_w, w_pm,
     conv_dw_w, conv_dw_b, conv_ln_w, conv_ln_b, w_pc, w_pa, sb_bias, w_out,
     router_w, router_b, exp_w1, exp_b1, exp_w2, exp_b2, final_w) = weights
    B, L, _ = x.shape
    past_len = 0 if page_table is None else page_table.shape[1] * PAGE_SIZE
    ks, vs, Cs, ns, ms, bufs = [], [], [], [], [], []
    for l in range(DEPTH):
        sh1, sc1, g1, sh2, sc2, g2 = _adaln(c, w_ada[l], b_ada[l])
        h = _rmsnorm(x, norm1_w[l]) * (1 + sc1) + sh1
        if page_table is None:
            k_past = jnp.zeros((B, 0, NH_SB, DH_SB), x.dtype)
            v_past = k_past
        else:
            k_past = cache_k[l][page_table].reshape(B, past_len, NH_SB, DH_SB)
            v_past = cache_v[l][page_table].reshape(B, past_len, NH_SB, DH_SB)
        mix, k_new, v_new, C, n, m, buf = _token_mixer(
            h, mC0[l], mn0[l], mm0[l], conv0[l], k_past, v_past, past_len,
            w_in[l], mlstm_gate_b[l], mlstm_norm_w[l], w_pm[l], conv_dw_w[l], conv_dw_b[l],
            conv_ln_w[l], conv_ln_b[l], w_pc[l], w_pa[l], sb_bias[l], w_out[l])
        x = x + g1 * mix
        h2 = _rmsnorm(x, norm2_w[l]) * (1 + sc2) + sh2
        ff = _moe(h2.reshape(B * L, D_MODEL), router_w[l], router_b[l],
                  exp_w1[l], exp_b1[l], exp_w2[l], exp_b2[l]).reshape(B, L, D_MODEL)
        x = x + g2 * ff
        ks.append(k_new)
        vs.append(v_new)
        Cs.append(C)
        ns.append(n)
        ms.append(m)
        bufs.append(buf)
    y = _rmsnorm(x, final_w)
    return y, jnp.stack(ks), jnp.stack(vs), jnp.stack(Cs), jnp.stack(ns), jnp.stack(ms), jnp.stack(bufs)


def setup_inputs(seed: int = 0) -> dict:
    key = jax.random.key(seed)
    ks = iter(jax.random.split(key, 40))
    f32 = jnp.float32

    def nrm(shape, scale):
        return jax.random.normal(next(ks), shape, f32) * scale

    n_pages = PAST_LEN // PAGE_SIZE
    n_phys = (DEC_BATCH * n_pages * 5) // 4
    x_prompt = nrm((BATCH, SEQ, D_MODEL), 1.0)
    x_sample = nrm((DEC_BATCH, DEC_SEQ, D_MODEL), 1.0)
    c_prompt = nrm((BATCH, D_MODEL), 1.0)
    c_sample = nrm((DEC_BATCH, D_MODEL), 1.0)
    cache_k = nrm((DEPTH, n_phys, PAGE_SIZE, NH_SB, DH_SB), 1.0)
    cache_v = nrm((DEPTH, n_phys, PAGE_SIZE, NH_SB, DH_SB), 1.0)
    state_C = nrm((DEPTH, DEC_BATCH, NH_M, DH_M, DH_M), 0.3)
    state_n = nrm((DEPTH, DEC_BATCH, NH_M, DH_M), 0.3)
    state_m = nrm((DEPTH, DEC_BATCH, NH_M), 1.0)
    state_conv = nrm((DEPTH, DEC_BATCH, CONV_W - 1, CONV_CH), 0.5)
    page_table = jax.random.permutation(next(ks), n_phys)[:DEC_BATCH * n_pages].reshape(DEC_BATCH, n_pages).astype(jnp.int32)
    w_ada = nrm((DEPTH, D_MODEL, 6 * D_MODEL), 0.5 * D_MODEL ** -0.5)
    b_ada = nrm((DEPTH, 6 * D_MODEL), 0.02)
    norm1_w = 1.0 + nrm((DEPTH, D_MODEL), 0.02)
    norm2_w = 1.0 + nrm((DEPTH, D_MODEL), 0.02)
    w_in = nrm((DEPTH, D_MODEL, D_IN), D_MODEL ** -0.5)
    i_bias = nrm((DEPTH, NH_M), 0.1)
    f_bias = 3.0 + 3.0 * jax.random.uniform(next(ks), (DEPTH, NH_M), f32)
    mlstm_gate_b = jnp.concatenate([i_bias, f_bias], axis=-1)
    mlstm_norm_w = 1.0 + nrm((DEPTH, W_M), 0.02)
    w_pm = nrm((DEPTH, W_M, D_MODEL), W_M ** -0.5)
    conv_dw_w = nrm((DEPTH, CONV_W, CONV_CH), CONV_W ** -0.5)
    conv_dw_b = nrm((DEPTH, CONV_CH), 0.02)
    conv_ln_w = 1.0 + nrm((DEPTH, CONV_CH), 0.02)
    conv_ln_b = nrm((DEPTH, CONV_CH), 0.02)
    w_pc = nrm((DEPTH, CONV_CH, D_MODEL), CONV_CH ** -0.5)
    w_pa = nrm((DEPTH, W_SB, D_MODEL), W_SB ** -0.5)
    sb_bias = SB_BIAS_INIT + nrm((DEPTH, NH_SB), 1.0)
    w_out = nrm((DEPTH, D_MODEL, D_MODEL), D_MODEL ** -0.5)
    router_w = nrm((DEPTH, D_MODEL, N_EXP), D_MODEL ** -0.5)
    router_b = nrm((DEPTH, N_EXP), 0.01)
    exp_w1 = nrm((DEPTH, N_EXP, D_MODEL, 2 * D_FF), D_MODEL ** -0.5)
    exp_b1 = nrm((DEPTH, N_EXP, 2 * D_FF), 0.02)
    exp_w2 = nrm((DEPTH, N_EXP, D_FF, D_MODEL), D_FF ** -0.5)
    exp_b2 = nrm((DEPTH, N_EXP, D_MODEL), 0.02)
    final_w = 1.0 + nrm((D_MODEL,), 0.02)
    return {'x_prompt': x_prompt, 'x_sample': x_sample, 'c_prompt': c_prompt, 'c_sample': c_sample,
            'cache_k': cache_k, 'cache_v': cache_v, 'state_C': state_C, 'state_n': state_n,
            'state_m': state_m, 'state_conv': state_conv, 'page_table': page_table,
            'w_ada': w_ada, 'b_ada': b_ada, 'norm1_w': norm1_w, 'norm2_w': norm2_w, 'w_in': w_in,
            'mlstm_gate_b': mlstm_gate_b, 'mlstm_norm_w': mlstm_norm_w, 'w_pm': w_pm,
            'conv_dw_w': conv_dw_w, 'conv_dw_b': conv_dw_b, 'conv_ln_w': conv_ln_w, 'conv_ln_b': conv_ln_b,
            'w_pc': w_pc, 'w_pa': w_pa, 'sb_bias': sb_bias, 'w_out': w_out, 'router_w': router_w,
            'router_b': router_b, 'exp_w1': exp_w1, 'exp_b1': exp_b1, 'exp_w2': exp_w2, 'exp_b2': exp_b2,
            'final_w': final_w}


def reference(x_prompt, x_sample, c_prompt, c_sample, cache_k, cache_v, state_C, state_n, state_m,
              state_conv, page_table, w_ada, b_ada, norm1_w, norm2_w, w_in, mlstm_gate_b, mlstm_norm_w,
              w_pm, conv_dw_w, conv_dw_b, conv_ln_w, conv_ln_b, w_pc, w_pa, sb_bias, w_out, router_w,
              router_b, exp_w1, exp_b1, exp_w2, exp_b2, final_w):
    weights = (w_ada, b_ada, norm1_w, norm2_w, w_in, mlstm_gate_b, mlstm_norm_w, w_pm,
               conv_dw_w, conv_dw_b, conv_ln_w, conv_ln_b, w_pc, w_pa, sb_bias, w_out,
               router_w, router_b, exp_w1, exp_b1, exp_w2, exp_b2, final_w)
    B = x_prompt.shape[0]
    zC = jnp.zeros((DEPTH, B, NH_M, DH_M, DH_M), jnp.float32)
    zn = jnp.zeros((DEPTH, B, NH_M, DH_M), jnp.float32)
    zm = jnp.zeros((DEPTH, B, NH_M), jnp.float32)
    zconv = jnp.zeros((DEPTH, B, CONV_W - 1, CONV_CH), x_prompt.dtype)
    y_prompt, k_prompt, v_prompt, C_prompt, n_prompt, m_prompt, conv_prompt = _trunk(
        x_prompt, c_prompt, zC, zn, zm, zconv, None, None, None, weights)
    y_sample, k_sample, v_sample, C_sample, n_sample, m_sample, conv_sample = _trunk(
        x_sample, c_sample, state_C, state_n, state_m, state_conv, cache_k, cache_v, page_table, weights)
    return (y_prompt, y_sample, k_prompt, v_prompt, C_prompt, n_prompt, m_prompt, conv_prompt,
            k_sample, v_sample, C_sample, n_sample, m_sample, conv_sample)
```

```python
import functools

import jax
import jax.numpy as jnp
from jax import lax
from jax.experimental import pallas as pl
from jax.experimental.pallas import tpu as pltpu

F32 = jnp.float32
BF16 = jnp.bfloat16

D_MODEL = 1024
DEPTH = 4
NH_M = 4
DH_M = 128
W_M = NH_M * DH_M
M_CHUNK = 128
CONV_CH = 512
CONV_W = 31
CONV_HIST = 32
NH_SB = 8
DH_SB = 64
W_SB = NH_SB * DH_SB
PAGE_SIZE = 128
N_EXP = 32
TOP_K = 4
D_FF = D_MODEL
SWIGLU_ALPHA = 1.702
SWIGLU_LIMIT = 7.0
EPS = 1e-5
LANES = 128
NEG_BIG = -1e30

OFF_MQ = 0
OFF_MIF = 4 * W_M
OFF_CG = OFF_MIF + 2 * NH_M
OFF_SQ = OFF_CG + 2 * CONV_CH
OFF_G = OFF_SQ + 3 * W_SB
D_IN = OFF_G + 3 * D_MODEL

U_M = 0
U_CG = 2048
U_G = 3072
U_S = 6144
U_W = 7680
TN_IN = 512
J_SK = (U_S + W_SB) // TN_IN

VMEM_LIMIT = 56 * 1024 * 1024


def _cparams(sem):
    return pltpu.CompilerParams(dimension_semantics=sem, vmem_limit_bytes=VMEM_LIMIT)


def _log_sigmoid(x):
    return jnp.minimum(x, 0.0) - jnp.log(1.0 + jnp.exp(-jnp.abs(x)))


def _sigmoid(x):
    return 1.0 / (1.0 + jnp.exp(-x))


def _dot(a, b):
    return jnp.dot(a, b, preferred_element_type=F32)


def _dot_nt(a, b):
    return lax.dot_general(a, b, (((1,), (1,)), ((), ())), preferred_element_type=F32)


def _dot_tn(a, b):
    return lax.dot_general(a, b, (((0,), (0,)), ((), ())), preferred_element_type=F32)


def _split_bf16(x):
    hi = x.astype(BF16)
    lo = (x - hi.astype(F32)).astype(BF16)
    return hi, lo


def _adaln_kernel(c_ref, w_ref, b_ref, o_ref):
    c = c_ref[...]
    s = c * _sigmoid(c)
    s_hi, s_lo = _split_bf16(s)
    w_hi, w_lo = _split_bf16(w_ref[0])
    o_ref[0] = _dot(s_hi, w_hi) + _dot(s_lo, w_hi) + _dot(s_hi, w_lo) + b_ref[0]


def _adaln(c_all, w_ada, b_ada):
    nb = c_all.shape[0]
    tn = 1536
    return pl.pallas_call(
        _adaln_kernel,
        grid=(DEPTH, 6 * D_MODEL // tn),
        in_specs=[
            pl.BlockSpec((nb, D_MODEL), lambda l, j: (0, 0)),
            pl.BlockSpec((1, D_MODEL, tn), lambda l, j: (l, 0, j)),
            pl.BlockSpec((1, 1, tn), lambda l, j: (l, 0, j)),
        ],
        out_specs=pl.BlockSpec((1, nb, tn), lambda l, j: (l, 0, j)),
        out_shape=jax.ShapeDtypeStruct((DEPTH, nb, 6 * D_MODEL), F32),
        compiler_params=_cparams(("arbitrary", "arbitrary")),
    )(c_all, w_ada, b_ada.reshape(DEPTH, 1, 6 * D_MODEL))


def _in_proj_kernel(has_ff, *refs):
    if has_ff:
        (x_ref, ff_ref, g2_ref, sh_ref, sc_ref, nw_ref, w_ref, wif_ref, bif_ref,
         u_ref, kv_ref, if_ref, xo_ref, h_scr) = refs
    else:
        (x_ref, sh_ref, sc_ref, nw_ref, w_ref, wif_ref, bif_ref,
         u_ref, kv_ref, if_ref, h_scr) = refs
    j = pl.program_id(2)

    @pl.when(j == 0)
    def _():
        x = x_ref[0]
        if has_ff:
            x = x + g2_ref[0] * ff_ref[0]
            xo_ref[0] = x
        ms = jnp.mean(x * x, axis=-1, keepdims=True)
        h = x * lax.rsqrt(ms + EPS) * nw_ref[...]
        h = h * (1.0 + sc_ref[0]) + sh_ref[0]
        h16 = h.astype(BF16)
        h_scr[...] = h16
        if_ref[0] = _dot(h16, wif_ref[...]) + bif_ref[...]

    u = _dot(h_scr[...], w_ref[...])
    u_ref[0] = u.astype(BF16)

    @pl.when((j >= J_SK) & (j < J_SK + 2))
    def _():
        kv_ref[0] = u


def _in_proj(x, ff, g2, sh, sc, nw, w16, wif16, bif, tl):
    B, L, D = x.shape
    lm = sh.shape[1]
    has_ff = ff is not None
    tlm = 1 if lm == 1 else tl
    row = lambda b, i, j: (b, i, 0)
    mod = (lambda b, i, j: (b, 0, 0)) if lm == 1 else row
    fixed = lambda b, i, j: (0, 0)
    in_specs = [pl.BlockSpec((1, tl, D), row)]
    args = [x]
    if has_ff:
        in_specs += [pl.BlockSpec((1, tl, D), row), pl.BlockSpec((1, tlm, D), mod)]
        args += [ff, g2]
    in_specs += [
        pl.BlockSpec((1, tlm, D), mod),
        pl.BlockSpec((1, tlm, D), mod),
        pl.BlockSpec((1, D), fixed),
        pl.BlockSpec((D, TN_IN), lambda b, i, j: (0, j)),
        pl.BlockSpec((D, LANES), fixed),
        pl.BlockSpec((1, LANES), fixed),
    ]
    args += [sh, sc, nw, w16, wif16, bif]
    out_specs = [
        pl.BlockSpec((1, tl, TN_IN), lambda b, i, j: (b, i, j)),
        pl.BlockSpec((1, tl, W_SB), lambda b, i, j: (b, i, jnp.clip(j - J_SK, 0, 1))),
        pl.BlockSpec((1, tl, LANES), row),
    ]
    out_shape = [
        jax.ShapeDtypeStruct((B, L, U_W), BF16),
        jax.ShapeDtypeStruct((B, L, 2 * W_SB), F32),
        jax.ShapeDtypeStruct((B, L, LANES), F32),
    ]
    if has_ff:
        out_specs.append(pl.BlockSpec((1, tl, D), row))
        out_shape.append(jax.ShapeDtypeStruct((B, L, D), F32))
    return pl.pallas_call(
        functools.partial(_in_proj_kernel, has_ff),
        grid=(B, L // tl, U_W // TN_IN),
        in_specs=in_specs,
        out_specs=out_specs,
        out_shape=out_shape,
        scratch_shapes=[pltpu.VMEM((tl, D), BF16)],
        compiler_params=_cparams(("arbitrary", "arbitrary", "arbitrary")),
    )(*args)


def _mlstm_kernel(valid, q_ref, k_ref, v_ref, o_ref, if_ref, c0_ref, n0_ref, m0_ref, nw_ref,
                  h_ref, co_ref, no_ref, mo_ref, c_scr, n_scr, m_scr, gt_scr):
    hd = pl.program_id(1)
    c = pl.program_id(2)
    cs = M_CHUNK

    @pl.when(c == 0)
    def _():
        c_scr[...] = c0_ref[0, 0]
        n_scr[...] = n0_ref[0, 0]
        m_scr[...] = m0_ref[0, 0]

    q = q_ref[0]
    ksc = k_ref[0].astype(F32) * (DH_M ** -0.5)
    kb = ksc.astype(BF16)
    v = v_ref[0]
    gates = if_ref[0]
    gt_scr[...] = gates.T
    lane = lax.broadcasted_iota(jnp.int32, (cs, LANES), 1)
    i_col = jnp.sum(jnp.where(lane == hd, gates, 0.0), axis=1, keepdims=True)
    f_col = jnp.sum(jnp.where(lane == hd + NH_M, gates, 0.0), axis=1, keepdims=True)
    i_row = gt_scr[pl.ds(hd, 1), :]
    f_row = gt_scr[pl.ds(hd + NH_M, 1), :]
    lf_col = _log_sigmoid(f_col)
    lf_row = _log_sigmoid(f_row)
    t_idx = lax.broadcasted_iota(jnp.int32, (cs, cs), 0)
    s_idx = lax.broadcasted_iota(jnp.int32, (cs, cs), 1)
    if valid < cs:
        col_ok = t_idx[:, :1] < valid
        row_ok = s_idx[:1, :] < valid
        i_col = jnp.where(col_ok, i_col, NEG_BIG)
        lf_col = jnp.where(col_ok, lf_col, 0.0)
        i_row = jnp.where(row_ok, i_row, NEG_BIG)
        lf_row = jnp.where(row_ok, lf_row, 0.0)
    causal = s_idx <= t_idx
    F_col = jnp.sum(jnp.where(causal, lf_row, 0.0), axis=1, keepdims=True)
    F_row = jnp.sum(jnp.where(t_idx <= s_idx, lf_col, 0.0), axis=0, keepdims=True)
    F_end = jnp.sum(lf_row, axis=1, keepdims=True)
    m_prev = m_scr[:, 0:1]

    dlog = jnp.where(causal, F_col - F_row + i_row, -jnp.inf)
    b_inter = F_col + m_prev
    m_t = jnp.maximum(b_inter, jnp.max(dlog, axis=1, keepdims=True))
    s = _dot_nt(q, kb) * jnp.exp(dlog - m_t)
    a_inter = jnp.exp(b_inter - m_t)
    Cmat = c_scr[...]
    nvec = n_scr[...]
    num = _dot(s.astype(BF16), v) + a_inter * _dot_nt(q, Cmat.astype(BF16))
    qn = jnp.sum(q.astype(F32) * nvec, axis=1, keepdims=True)
    den = jnp.sum(s, axis=1, keepdims=True) + a_inter * qn
    den = jnp.maximum(jnp.abs(den), jnp.exp(-m_t))
    h = num / den

    dec_row = F_end - F_row + i_row
    dec_col = F_end - F_col + i_col
    m_new = jnp.maximum(F_end + m_prev, jnp.max(dec_row, axis=1, keepdims=True))
    w_col = jnp.exp(dec_col - m_new)
    keep = jnp.exp(F_end + m_prev - m_new)
    c_scr[...] = keep * Cmat + _dot_tn((w_col * v.astype(F32)).astype(BF16), kb)
    n_scr[...] = keep * nvec + jnp.sum(w_col * ksc, axis=0, keepdims=True)
    m_scr[...] = jnp.broadcast_to(m_new, m_scr.shape)

    mu = jnp.mean(h, axis=-1, keepdims=True)
    var = jnp.mean(jnp.square(h - mu), axis=-1, keepdims=True)
    hn = (h - mu) * lax.rsqrt(var + EPS) * nw_ref[0]
    h_ref[0] = (hn * _sigmoid(o_ref[0].astype(F32))).astype(BF16)

    @pl.when(c == pl.num_programs(2) - 1)
    def _():
        co_ref[0, 0] = c_scr[...]
        no_ref[0, 0] = n_scr[...]
        mo_ref[0, 0] = m_scr[...]


def _mlstm(um, if32, C0, n0, m0, mnorm_w, valid):
    B, L = um.shape[0], um.shape[1]
    nc = L // M_CHUNK
    blk = lambda off: pl.BlockSpec((1, M_CHUNK, DH_M), lambda b, h, c: (b, c, off + h))
    st = lambda r: pl.BlockSpec((1, 1, r, DH_M), lambda b, h, c: (b, h, 0, 0))
    return pl.pallas_call(
        functools.partial(_mlstm_kernel, valid),
        grid=(B, NH_M, nc),
        in_specs=[
            blk(0), blk(NH_M), blk(2 * NH_M), blk(3 * NH_M),
            pl.BlockSpec((1, M_CHUNK, LANES), lambda b, h, c: (b, c, 0)),
            st(DH_M), st(1), st(1),
            pl.BlockSpec((1, 1, DH_M), lambda b, h, c: (h, 0, 0)),
        ],
        out_specs=[
            pl.BlockSpec((1, M_CHUNK, DH_M), lambda b, h, c: (b, c, h)),
            st(DH_M), st(1), st(1),
        ],
        out_shape=[
            jax.ShapeDtypeStruct((B, L, W_M), BF16),
            jax.ShapeDtypeStruct((B, NH_M, DH_M, DH_M), F32),
            jax.ShapeDtypeStruct((B, NH_M, 1, DH_M), F32),
            jax.ShapeDtypeStruct((B, NH_M, 1, DH_M), F32),
        ],
        scratch_shapes=[
            pltpu.VMEM((DH_M, DH_M), F32),
            pltpu.VMEM((1, DH_M), F32),
            pltpu.VMEM((1, DH_M), F32),
            pltpu.VMEM((LANES, M_CHUNK), F32),
        ],
        compiler_params=_cparams(("arbitrary", "arbitrary", "arbitrary")),
    )(um, um, um, um, if32, C0, n0.reshape(B, NH_M, 1, DH_M),
      jnp.broadcast_to(m0[:, :, None, None], (B, NH_M, 1, DH_M)),
      mnorm_w.reshape(NH_M, 1, DH_M))


def _conv_kernel(tl, rb, a_ref, g_ref, st_ref, w_ref, b_ref, lw_ref, lb_ref,
                 o_ref, so_ref, buf, shifted):
    i = pl.program_id(1)
    sub = 8

    @pl.when(i == 0)
    def _():
        buf[0:CONV_HIST, :] = st_ref[0]

    a = a_ref[0].astype(F32) * _sigmoid(g_ref[0].astype(F32))
    buf[CONV_HIST:CONV_HIST + tl, :] = a
    pad = CONV_HIST - (CONV_W - 1)
    span = shifted.shape[1]
    for s in range(1, sub):
        shifted[s - 1] = buf[s:s + span, :]

    def block(r, carry):
        base = pl.multiple_of(r * rb, rb)
        acc = jnp.broadcast_to(b_ref[...], (rb, CONV_CH))
        for j in range(CONV_W):
            qq, s = divmod(j + pad, sub)
            start = pl.multiple_of(base + sub * qq, sub)
            if s == 0:
                rows = buf[pl.ds(start, rb), :]
            else:
                rows = shifted[s - 1, pl.ds(start, rb), :]
            acc = acc + w_ref[j:j + 1, :] * rows
        mu = jnp.mean(acc, axis=-1, keepdims=True)
        var = jnp.mean(jnp.square(acc - mu), axis=-1, keepdims=True)
        y = (acc - mu) * lax.rsqrt(var + EPS) * lw_ref[...] + lb_ref[...]
        o_ref[0, pl.ds(base, rb), :] = (y * _sigmoid(y)).astype(o_ref.dtype)
        return carry

    lax.fori_loop(0, tl // rb, block, 0)
    tail = buf[tl:tl + CONV_HIST, :]
    buf[0:CONV_HIST, :] = tail

    @pl.when(i == pl.num_programs(1) - 1)
    def _():
        so_ref[0] = tail


def _conv(ua, ug, off_a, off_g, state, dw_w, dw_b, ln_w, ln_b, tl):
    B, L = ua.shape[0], ua.shape[1]
    rb = min(tl, 32)
    pad = CONV_HIST - (CONV_W - 1)
    st = jnp.pad(state, ((0, 0), (pad, 0), (0, 0)))
    w = jnp.pad(dw_w, ((0, CONV_HIST - CONV_W), (0, 0)))
    vec = lambda b, i: (0, 0)
    out, so = pl.pallas_call(
        functools.partial(_conv_kernel, tl, rb),
        grid=(B, L // tl),
        in_specs=[
            pl.BlockSpec((1, tl, CONV_CH), lambda b, i: (b, i, off_a)),
            pl.BlockSpec((1, tl, CONV_CH), lambda b, i: (b, i, off_g)),
            pl.BlockSpec((1, CONV_HIST, CONV_CH), lambda b, i: (b, 0, 0)),
            pl.BlockSpec((CONV_HIST, CONV_CH), vec),
            pl.BlockSpec((1, CONV_CH), vec),
            pl.BlockSpec((1, CONV_CH), vec),
            pl.BlockSpec((1, CONV_CH), vec),
        ],
        out_specs=[
            pl.BlockSpec((1, tl, CONV_CH), lambda b, i: (b, i, 0)),
            pl.BlockSpec((1, CONV_HIST, CONV_CH), lambda b, i: (b, 0, 0)),
        ],
        out_shape=[
            jax.ShapeDtypeStruct((B, L, CONV_CH), ua.dtype),
            jax.ShapeDtypeStruct((B, CONV_HIST, CONV_CH), F32),
        ],
        scratch_shapes=[pltpu.VMEM((CONV_HIST + tl, CONV_CH), F32),
                        pltpu.VMEM((7, CONV_HIST + tl - 8, CONV_CH), F32)],
        compiler_params=_cparams(("arbitrary", "arbitrary")),
    )(ua, ug, st, w, dw_b.reshape(1, -1), ln_w.reshape(1, -1), ln_b.reshape(1, -1))
    return out, so[:, pad:, :]


def _sb_tri():
    r = lax.broadcasted_iota(jnp.int32, (LANES, 2 * LANES), 0)
    c = lax.broadcasted_iota(jnp.int32, (LANES, 2 * LANES), 1)
    return jnp.where((c >= LANES) | (r > c), 1.0, 0.0).astype(BF16)


def _sb_prompt_kernel(bias_ref, q_ref, k_ref, v_ref, o_ref, acc_scr, car_scr):
    p = pl.program_id(1)
    qi = pl.program_id(2)
    tq = LANES
    lane = lax.broadcasted_iota(jnp.int32, (1, LANES), 1)
    q2 = q_ref[0].astype(F32) * (DH_SB ** -0.5)
    qm = [jnp.where((lane // DH_SB) == hh, q2, 0.0).astype(BF16) for hh in range(2)]
    tri = _sb_tri()
    acc_scr[...] = jnp.zeros_like(acc_scr)
    car_scr[...] = jnp.zeros_like(car_scr)
    rr = lax.broadcasted_iota(jnp.int32, (tq, LANES), 0)
    cc = lax.broadcasted_iota(jnp.int32, (tq, LANES), 1)

    def body(it, carry):
        kb = qi - it
        start = pl.multiple_of(kb * LANES, LANES)
        kblk = k_ref[0, pl.ds(start, LANES), :]
        vblk = v_ref[0, pl.ds(start, LANES), :]
        visible = (cc + kb * LANES) < (rr + qi * tq)
        for hh in range(2):
            z = _dot_nt(qm[hh], kblk) + bias_ref[2 * p + hh]
            sp = jnp.log(1.0 + jnp.exp(-jnp.abs(z)))
            log_beta = jnp.minimum(z, 0.0) - sp
            log_1m = jnp.where(visible, log_beta - z, 0.0)
            hi, lo = _split_bf16(log_1m)
            cum = _dot(hi, tri) + _dot(lo, tri)
            car = car_scr[hh]
            A = jnp.where(visible, jnp.exp(log_beta + cum[:, :LANES] + car), 0.0)
            car_scr[hh] = car + cum[:, LANES:]
            acc_scr[hh] = acc_scr[hh] + _dot(A.astype(BF16), vblk)
        return carry

    lax.fori_loop(0, qi + 1, body, 0)
    o_ref[0] = jnp.where((cc // DH_SB) == 0, acc_scr[0], acc_scr[1]).astype(BF16)


def _sb_prompt(u16, sb_b):
    B, L = u16.shape[0], u16.shape[1]
    cb = U_S // LANES
    return pl.pallas_call(
        _sb_prompt_kernel,
        grid=(B, NH_SB // 2, L // LANES),
        in_specs=[
            pl.BlockSpec(memory_space=pltpu.SMEM),
            pl.BlockSpec((1, LANES, LANES), lambda b, p, i: (b, i, cb + p)),
            pl.BlockSpec((1, L, LANES), lambda b, p, i: (b, 0, cb + 4 + p)),
            pl.BlockSpec((1, L, LANES), lambda b, p, i: (b, 0, cb + 8 + p)),
        ],
        out_specs=pl.BlockSpec((1, LANES, LANES), lambda b, p, i: (b, i, p)),
        out_shape=jax.ShapeDtypeStruct((B, L, W_SB), BF16),
        scratch_shapes=[pltpu.VMEM((2, LANES, LANES), F32), pltpu.VMEM((2, LANES, LANES), F32)],
        compiler_params=_cparams(("arbitrary", "arbitrary", "arbitrary")),
    )(sb_b, u16, u16, u16)


PAGES_PER_STEP = 8


def _sb_sample_kernel(lq, pt_ref, qbd_ref, bias_ref, kn_ref, vn_ref, *refs):
    g = PAGES_PER_STEP
    k_refs = refs[:g]
    v_refs = refs[g:2 * g]
    o_ref = refs[2 * g]
    acc_scr, car_scr = refs[2 * g + 1:]
    s = pl.program_id(1)
    nl = NH_SB * lq
    qbd = qbd_ref[0]
    bias = bias_ref[...]

    def tri(n):
        r = lax.broadcasted_iota(jnp.int32, (n, n), 0)
        c = lax.broadcasted_iota(jnp.int32, (n, n), 1)
        return jnp.where(c > r, 1.0, 0.0).astype(BF16)

    def block(kblk, vblk, visible, car):
        n = kblk.shape[0]
        z = _dot(kblk.astype(BF16), qbd) + bias
        sp = jnp.log(1.0 + jnp.exp(-jnp.abs(z)))
        log_beta = jnp.minimum(z, 0.0) - sp
        log_1m = log_beta - z
        if visible is not None:
            log_1m = jnp.where(visible, log_1m, 0.0)
        hi, lo = _split_bf16(log_1m)
        t = tri(n)
        later = _dot(t, hi) + _dot(t, lo)
        A = jnp.exp(log_beta + later + car)
        if visible is not None:
            A = jnp.where(visible, A, 0.0)
        car = car + jnp.sum(log_1m, axis=0, keepdims=True)
        return _dot_tn(A.astype(BF16), vblk.astype(BF16)), car

    @pl.when(s == 0)
    def _():
        kn = kn_ref[0]
        r = lax.broadcasted_iota(jnp.int32, (kn.shape[0], nl), 0)
        c = lax.broadcasted_iota(jnp.int32, (kn.shape[0], nl), 1)
        upd, car = block(kn, vn_ref[0], r < (c % lq), jnp.zeros((1, nl), F32))
        acc_scr[...] = upd
        car_scr[...] = car

    car = car_scr[...]
    acc = acc_scr[...]
    for gi in range(g):
        upd, car = block(k_refs[gi][0], v_refs[gi][0], None, car)
        acc = acc + upd
    acc_scr[...] = acc
    car_scr[...] = car

    @pl.when(s == pl.num_programs(1) - 1)
    def _():
        r = lax.broadcasted_iota(jnp.int32, (nl, W_SB), 0)
        c = lax.broadcasted_iota(jnp.int32, (nl, W_SB), 1)
        m = jnp.where((r // lq) == (c // DH_SB), acc, 0.0)
        out = m[0:lq]
        for hh in range(1, NH_SB):
            out = out + m[hh * lq:(hh + 1) * lq]
        o_ref[0] = out


def _sb_sample(q16, k_new, v_new, cache_k, cache_v, page_table, sb_b):
    B, lq, _ = q16.shape
    n_pages = page_table.shape[1]
    g = PAGES_PER_STEP
    nl = NH_SB * lq
    k_new = jnp.pad(k_new, ((0, 0), (0, PAGE_SIZE - lq), (0, 0)))
    v_new = jnp.pad(v_new, ((0, 0), (0, PAGE_SIZE - lq), (0, 0)))
    qh = (q16.astype(F32) * (DH_SB ** -0.5)).reshape(B, lq, NH_SB, DH_SB)
    qbd = jnp.einsum('bihd,hg->bhdgi', qh, jnp.eye(NH_SB, dtype=F32)).reshape(B, W_SB, nl).astype(BF16)
    bias = jnp.repeat(sb_b.astype(F32), lq).reshape(1, nl)

    def page_spec(gi):
        return pl.BlockSpec((1, PAGE_SIZE, W_SB),
                            lambda b, s, pt: (pt[b, n_pages - 1 - (s * g + gi)], 0, 0))

    grid_spec = pltpu.PrefetchScalarGridSpec(
        num_scalar_prefetch=1,
        grid=(B, n_pages // g),
        in_specs=[
            pl.BlockSpec((1, W_SB, nl), lambda b, s, pt: (b, 0, 0)),
            pl.BlockSpec((1, nl), lambda b, s, pt: (0, 0)),
            pl.BlockSpec((1, PAGE_SIZE, W_SB), lambda b, s, pt: (b, 0, 0)),
            pl.BlockSpec((1, PAGE_SIZE, W_SB), lambda b, s, pt: (b, 0, 0)),
        ] + [page_spec(gi) for gi in range(g)] * 2,
        out_specs=pl.BlockSpec((1, lq, W_SB), lambda b, s, pt: (b, 0, 0)),
        scratch_shapes=[pltpu.VMEM((nl, W_SB), F32), pltpu.VMEM((1, nl), F32)],
    )
    return pl.pallas_call(
        functools.partial(_sb_sample_kernel, lq),
        grid_spec=grid_spec,
        out_shape=jax.ShapeDtypeStruct((B, lq, W_SB), F32),
        compiler_params=_cparams(("arbitrary", "arbitrary")),
    )(page_table, qbd, bias, k_new, v_new, *([cache_k] * g), *([cache_v] * g))


def _merge_kernel(hm_ref, hc_ref, ha_ref, gm_ref, gc_ref, ga_ref, x_ref, g1_ref, sh_ref, sc_ref,
                  wpm_ref, wpc_ref, wpa_ref, wo_ref, nw_ref, rwh_ref, rwl_ref, rb_ref,
                  xo_ref, h2_ref, gate_ref):
    y = _sigmoid(gm_ref[0].astype(F32)) * _dot(hm_ref[0], wpm_ref[...])
    y = y + _sigmoid(gc_ref[0].astype(F32)) * _dot(hc_ref[0], wpc_ref[...])
    y = y + _sigmoid(ga_ref[0].astype(F32)) * _dot(ha_ref[0], wpa_ref[...])
    mix = _dot(y.astype(BF16), wo_ref[...])
    x = x_ref[0] + g1_ref[0] * mix
    xo_ref[0] = x
    ms = jnp.mean(x * x, axis=-1, keepdims=True)
    h = x * lax.rsqrt(ms + EPS) * nw_ref[...]
    h = h * (1.0 + sc_ref[0]) + sh_ref[0]
    h_hi, h_lo = _split_bf16(h)
    h2_ref[0] = h_hi
    logits = (_dot(h_hi, rwh_ref[...]) + _dot(h_lo, rwh_ref[...]) + _dot(h_hi, rwl_ref[...])
              + rb_ref[...])
    lane = lax.broadcasted_iota(jnp.int32, logits.shape, 1).astype(F32)
    work = logits
    sel = jnp.zeros(logits.shape, F32)
    vmax = None
    for _ in range(TOP_K):
        mx = jnp.max(work, axis=-1, keepdims=True)
        if vmax is None:
            vmax = mx
        first = jnp.min(jnp.where(work == mx, lane, float(LANES)), axis=-1, keepdims=True)
        pick = lane == first
        sel = jnp.where(pick, 1.0, sel)
        work = jnp.where(pick, -jnp.inf, work)
    e = sel * jnp.exp(logits - vmax)
    gate_ref[0] = e / jnp.sum(e, axis=-1, keepdims=True)


def _merge(hm, hc, ha, u16, x, g1, sh2, sc2, wpm, wpc, wpa, wo, nw, rw_hi, rw_lo, rb, tl):
    B, L, D = x.shape
    lm = g1.shape[1]
    tlm = 1 if lm == 1 else tl
    row = lambda b, i: (b, i, 0)
    mod = (lambda b, i: (b, 0, 0)) if lm == 1 else row
    fixed = lambda b, i: (0, 0)
    gb = U_G // D
    half = lambda: pl.BlockSpec((1, tl, W_M), row)
    gate = lambda n: pl.BlockSpec((1, tl, D), lambda b, i: (b, i, gb + n))
    modspec = lambda: pl.BlockSpec((1, tlm, D), mod)
    return pl.pallas_call(
        _merge_kernel,
        grid=(B, L // tl),
        in_specs=[
            half(), half(), half(), gate(0), gate(1), gate(2),
            pl.BlockSpec((1, tl, D), row), modspec(), modspec(), modspec(),
            pl.BlockSpec((W_M, D), fixed), pl.BlockSpec((CONV_CH, D), fixed),
            pl.BlockSpec((W_SB, D), fixed), pl.BlockSpec((D, D), fixed),
            pl.BlockSpec((1, D), fixed),
            pl.BlockSpec((D, LANES), fixed), pl.BlockSpec((D, LANES), fixed),
            pl.BlockSpec((1, LANES), fixed),
        ],
        out_specs=[
            pl.BlockSpec((1, tl, D), row),
            pl.BlockSpec((1, tl, D), row),
            pl.BlockSpec((1, tl, LANES), row),
        ],
        out_shape=[
            jax.ShapeDtypeStruct((B, L, D), F32),
            jax.ShapeDtypeStruct((B, L, D), BF16),
            jax.ShapeDtypeStruct((B, L, LANES), F32),
        ],
        compiler_params=_cparams(("arbitrary", "arbitrary")),
    )(hm, hc, ha, u16, u16, u16, x, g1, sh2, sc2, wpm, wpc, wpa, wo, nw, rw_hi, rw_lo, rb)


def _moe_kernel(h_ref, gate_ref, w1_ref, b1_ref, w2_ref, b2_ref, o_ref):
    e = pl.program_id(1)

    @pl.when(e == 0)
    def _():
        o_ref[...] = jnp.zeros_like(o_ref)

    gates = gate_ref[...]
    lane = lax.broadcasted_iota(jnp.int32, gates.shape, 1)
    ge = jnp.sum(jnp.where(lane == e, gates, 0.0), axis=-1, keepdims=True)
    z = _dot(h_ref[...], w1_ref[0]) + b1_ref[0]
    z_glu = jnp.minimum(z[:, :D_FF], SWIGLU_LIMIT)
    z_lin = jnp.clip(z[:, D_FF:], -SWIGLU_LIMIT, SWIGLU_LIMIT)
    act = z_glu * _sigmoid(SWIGLU_ALPHA * z_glu) * (z_lin + 1.0)
    y = _dot(act.astype(BF16), w2_ref[0]) + b2_ref[0]
    o_ref[...] += ge * y


def _moe(h2, gates, w1, b1, w2, b2, tm):
    N, D = h2.shape
    return pl.pallas_call(
        _moe_kernel,
        grid=(N // tm, N_EXP),
        in_specs=[
            pl.BlockSpec((tm, D), lambda i, e: (i, 0)),
            pl.BlockSpec((tm, LANES), lambda i, e: (i, 0)),
            pl.BlockSpec((1, D, 2 * D_FF), lambda i, e: (e, 0, 0)),
            pl.BlockSpec((1, 1, 2 * D_FF), lambda i, e: (e, 0, 0)),
            pl.BlockSpec((1, D_FF, D), lambda i, e: (e, 0, 0)),
            pl.BlockSpec((1, 1, D), lambda i, e: (e, 0, 0)),
        ],
        out_specs=pl.BlockSpec((tm, D), lambda i, e: (i, 0)),
        out_shape=jax.ShapeDtypeStruct((N, D), F32),
        compiler_params=_cparams(("arbitrary", "arbitrary")),
    )(h2, gates, w1, b1.reshape(N_EXP, 1, 2 * D_FF), w2, b2.reshape(N_EXP, 1, D))


def _final_kernel(x_ref, ff_ref, g2_ref, w_ref, o_ref):
    x = x_ref[0] + g2_ref[0] * ff_ref[0]
    ms = jnp.mean(x * x, axis=-1, keepdims=True)
    o_ref[0] = x * lax.rsqrt(ms + EPS) * w_ref[...]


def _final(x, ff, g2, w, tl):
    B, L, D = x.shape
    lm = g2.shape[1]
    tlm = 1 if lm == 1 else tl
    row = lambda b, i: (b, i, 0)
    mod = (lambda b, i: (b, 0, 0)) if lm == 1 else row
    return pl.pallas_call(
        _final_kernel,
        grid=(B, L // tl),
        in_specs=[
            pl.BlockSpec((1, tl, D), row), pl.BlockSpec((1, tl, D), row),
            pl.BlockSpec((1, tlm, D), mod), pl.BlockSpec((1, D), lambda b, i: (0, 0)),
        ],
        out_specs=pl.BlockSpec((1, tl, D), row),
        out_shape=jax.ShapeDtypeStruct((B, L, D), F32),
        compiler_params=_cparams(("arbitrary", "arbitrary")),
    )(x, ff, g2, w.reshape(1, D))


def _prep_layer_weights(l, w_in, mlstm_gate_b, router_w, router_b):
    wl = w_in[l]
    w_main = jnp.concatenate(
        [wl[:, OFF_MQ:OFF_MIF], wl[:, OFF_CG:OFF_SQ], wl[:, OFF_G:], wl[:, OFF_SQ:OFF_G]], axis=1)
    wif = jnp.pad(wl[:, OFF_MIF:OFF_CG], ((0, 0), (0, LANES - 2 * NH_M)))
    bif = jnp.pad(mlstm_gate_b[l], (0, LANES - 2 * NH_M)).reshape(1, LANES)
    rw = jnp.pad(router_w[l], ((0, 0), (0, LANES - N_EXP)))
    rw_hi = rw.astype(BF16)
    rw_lo = (rw - rw_hi.astype(F32)).astype(BF16)
    rb = jnp.pad(router_b[l], (0, LANES - N_EXP), constant_values=NEG_BIG).reshape(1, LANES)
    return w_main.astype(BF16), wif.astype(BF16), bif, rw_hi, rw_lo, rb


def _trunk(x, mods, C0, n0, m0, conv0, paged, W, tl, tl_mg, tm):
    B, L, D = x.shape
    sh1, sc1, g1, sh2, sc2, g2 = mods
    ks, vs, Cs, ns, ms, bufs = [], [], [], [], [], []
    ff = None
    for l in range(DEPTH):
        w16, wif16, bif, rw_hi, rw_lo, rb = W['prep'][l]
        nw1 = W['norm1_w'][l].reshape(1, D)
        if ff is None:
            u16, kv32, if32 = _in_proj(x, None, None, sh1[l], sc1[l], nw1, w16, wif16, bif, tl)
        else:
            u16, kv32, if32, x = _in_proj(x, ff, g2[l - 1], sh1[l], sc1[l], nw1, w16, wif16, bif, tl)
        if paged is None:
            hm, Cn, nn, mn = _mlstm(u16, if32, C0[l], n0[l], m0[l], W['mlstm_norm_w'][l], M_CHUNK)
            hc, buf = _conv(u16, u16, U_CG // CONV_CH, U_CG // CONV_CH + 1, conv0[l],
                            W['conv_dw_w'][l], W['conv_dw_b'][l], W['conv_ln_w'][l], W['conv_ln_b'][l],
                            min(512, L))
            ha = _sb_prompt(u16, W['sb_bias'][l])
            k_new = kv32[..., :W_SB].reshape(B, L, NH_SB, DH_SB)
            v_new = kv32[..., W_SB:].reshape(B, L, NH_SB, DH_SB)
        else:
            cache_k, cache_v, page_table, bs, ls = paged
            us = u16.reshape(bs, ls, U_W)
            padl = M_CHUNK - ls
            um = jnp.pad(us[..., :4 * W_M], ((0, 0), (0, padl), (0, 0)))
            ifp = jnp.pad(if32.reshape(bs, ls, LANES), ((0, 0), (0, padl), (0, 0)))
            hm, Cn, nn, mn = _mlstm(um, ifp, C0[l], n0[l], m0[l], W['mlstm_norm_w'][l], ls)
            hm = hm[:, :ls].reshape(B, L, W_M)
            ucg = us[..., U_CG:U_CG + 2 * CONV_CH].astype(F32)
            hc, buf = _conv(ucg, ucg, 0, 1, conv0[l],
                            W['conv_dw_w'][l], W['conv_dw_b'][l], W['conv_ln_w'][l], W['conv_ln_b'][l], ls)
            hc = hc.reshape(B, L, CONV_CH).astype(BF16)
            kvs = kv32.reshape(bs, ls, 2 * W_SB)
            ha = _sb_sample(us[..., U_S:U_S + W_SB], kvs[..., :W_SB], kvs[..., W_SB:],
                            cache_k[l].reshape(-1, PAGE_SIZE, W_SB), cache_v[l].reshape(-1, PAGE_SIZE, W_SB),
                            page_table, W['sb_bias'][l]).reshape(B, L, W_SB).astype(BF16)
            k_new = kvs[..., :W_SB].reshape(bs, ls, NH_SB, DH_SB)
            v_new = kvs[..., W_SB:].reshape(bs, ls, NH_SB, DH_SB)
        x, h2, gates = _merge(hm, hc, ha, u16, x, g1[l], sh2[l], sc2[l],
                              W['w_pm16'][l], W['w_pc16'][l], W['w_pa16'][l], W['w_out16'][l],
                              W['norm2_w'][l].reshape(1, D), rw_hi, rw_lo, rb, tl_mg)
        ff = _moe(h2.reshape(B * L, D), gates.reshape(B * L, LANES),
                  W['exp_w1_16'][l], W['exp_b1'][l], W['exp_w2_16'][l], W['exp_b2'][l], tm).reshape(B, L, D)
        ks.append(k_new)
        vs.append(v_new)
        Cs.append(Cn)
        ns.append(nn[:, :, 0, :])
        ms.append(mn[:, :, 0, 0])
        bufs.append(buf)
    y = _final(x, ff, g2[DEPTH - 1], W['final_w'], tl)
    return y, jnp.stack(ks), jnp.stack(vs), jnp.stack(Cs), jnp.stack(ns), jnp.stack(ms), jnp.stack(bufs)


def kernel(x_prompt, x_sample, c_prompt, c_sample, cache_k, cache_v, state_C, state_n, state_m, state_conv, page_table, w_ada, b_ada, norm1_w, norm2_w, w_in, mlstm_gate_b, mlstm_norm_w, w_pm, conv_dw_w, conv_dw_b, conv_ln_w, conv_ln_b, w_pc, w_pa, sb_bias, w_out, router_w, router_b, exp_w1, exp_b1, exp_w2, exp_b2, final_w):
    BP, LP, D = x_prompt.shape
    BS, LS, _ = x_sample.shape
    W = dict(norm1_w=norm1_w, norm2_w=norm2_w, mlstm_norm_w=mlstm_norm_w,
             conv_dw_w=conv_dw_w, conv_dw_b=conv_dw_b, conv_ln_w=conv_ln_w, conv_ln_b=conv_ln_b,
             sb_bias=sb_bias, exp_b1=exp_b1, exp_b2=exp_b2, final_w=final_w,
             w_pm16=w_pm.astype(BF16), w_pc16=w_pc.astype(BF16), w_pa16=w_pa.astype(BF16),
             w_out16=w_out.astype(BF16), exp_w1_16=exp_w1.astype(BF16), exp_w2_16=exp_w2.astype(BF16))
    W['prep'] = [_prep_layer_weights(l, w_in, mlstm_gate_b, router_w, router_b) for l in range(DEPTH)]

    nb = BP + BS
    nbp = -(-nb // 8) * 8
    c_all = jnp.pad(jnp.concatenate([c_prompt, c_sample], axis=0), ((0, nbp - nb), (0, 0)))
    mod = _adaln(c_all, w_ada, b_ada)
    mods_p = [mod[:, :BP, None, i * D:(i + 1) * D] for i in range(6)]
    mods_s = [jnp.repeat(mod[:, BP:nb, i * D:(i + 1) * D], LS, axis=1)[:, None] for i in range(6)]

    zC = jnp.zeros((DEPTH, BP, NH_M, DH_M, DH_M), F32)
    zn = jnp.zeros((DEPTH, BP, NH_M, DH_M), F32)
    zm = jnp.zeros((DEPTH, BP, NH_M), F32)
    zconv = jnp.zeros((DEPTH, BP, CONV_W - 1, CONV_CH), F32)
    yp, *outs_p = _trunk(x_prompt, mods_p, zC, zn, zm, zconv, None, W,
                         min(1024, LP), min(512, LP), min(1024, BP * LP))
    ys, *outs_s = _trunk(x_sample.reshape(1, BS * LS, D), mods_s, state_C, state_n, state_m, state_conv,
                         (cache_k, cache_v, page_table, BS, LS), W, BS * LS, BS * LS, BS * LS)
    return (yp, ys.reshape(BS, LS, D)) + tuple(outs_p) + tuple(outs_s)
```

```python
import functools

import jax
import jax.numpy as jnp
from jax import lax
from jax.experimental import pallas as pl
from jax.experimental.pallas import tpu as pltpu

F32 = jnp.float32
BF16 = jnp.bfloat16

D_MODEL = 1024
DEPTH = 4
NH_M = 4
DH_M = 128
W_M = NH_M * DH_M
M_CHUNK = 128
CONV_CH = 512
CONV_W = 31
CONV_HIST = 32
NH_SB = 8
DH_SB = 64
W_SB = NH_SB * DH_SB
PAGE_SIZE = 128
N_EXP = 32
TOP_K = 4
D_FF = D_MODEL
SWIGLU_ALPHA = 1.702
SWIGLU_LIMIT = 7.0
EPS = 1e-5
LANES = 128
NEG_BIG = -1e30

OFF_MQ = 0
OFF_MIF = 4 * W_M
OFF_CG = OFF_MIF + 2 * NH_M
OFF_SQ = OFF_CG + 2 * CONV_CH
OFF_G = OFF_SQ + 3 * W_SB
D_IN = OFF_G + 3 * D_MODEL

U_M = 0
U_CG = 2048
U_G = 3072
U_S = 6144
U_W = 7680
TN_IN = 512
J_SK = (U_S + W_SB) // TN_IN

VMEM_LIMIT = 56 * 1024 * 1024


def _cparams(sem):
    return pltpu.CompilerParams(dimension_semantics=sem, vmem_limit_bytes=VMEM_LIMIT)


def _log_sigmoid(x):
    return jnp.minimum(x, 0.0) - jnp.log(1.0 + jnp.exp(-jnp.abs(x)))


def _sigmoid(x):
    return 1.0 / (1.0 + jnp.exp(-x))


def _dot(a, b):
    return jnp.dot(a, b, preferred_element_type=F32)


def _dot_nt(a, b):
    return lax.dot_general(a, b, (((1,), (1,)), ((), ())), preferred_element_type=F32)


def _dot_tn(a, b):
    return lax.dot_general(a, b, (((0,), (0,)), ((), ())), preferred_element_type=F32)


def _split_bf16(x):
    hi = x.astype(BF16)
    lo = (x - hi.astype(F32)).astype(BF16)
    return hi, lo


def _adaln_kernel(c_ref, w_ref, b_ref, o_ref):
    c = c_ref[...]
    s = c * _sigmoid(c)
    s_hi, s_lo = _split_bf16(s)
    w_hi, w_lo = _split_bf16(w_ref[0])
    o_ref[0] = _dot(s_hi, w_hi) + _dot(s_lo, w_hi) + _dot(s_hi, w_lo) + b_ref[0]


def _adaln(c_all, w_ada, b_ada):
    nb = c_all.shape[0]
    tn = 1536
    return pl.pallas_call(
        _adaln_kernel,
        grid=(DEPTH, 6 * D_MODEL // tn),
        in_specs=[
            pl.BlockSpec((nb, D_MODEL), lambda l, j: (0, 0)),
            pl.BlockSpec((1, D_MODEL, tn), lambda l, j: (l, 0, j)),
            pl.BlockSpec((1, 1, tn), lambda l, j: (l, 0, j)),
        ],
        out_specs=pl.BlockSpec((1, nb, tn), lambda l, j: (l, 0, j)),
        out_shape=jax.ShapeDtypeStruct((DEPTH, nb, 6 * D_MODEL), F32),
        compiler_params=_cparams(("arbitrary", "arbitrary")),
    )(c_all, w_ada, b_ada.reshape(DEPTH, 1, 6 * D_MODEL))


def _in_proj_kernel(has_ff, *refs):
    if has_ff:
        (x_ref, ff_ref, g2_ref, sh_ref, sc_ref, nw_ref, w_ref, wif_ref, bif_ref,
         u_ref, kv_ref, if_ref, xo_ref, h_scr) = refs
    else:
        (x_ref, sh_ref, sc_ref, nw_ref, w_ref, wif_ref, bif_ref,
         u_ref, kv_ref, if_ref, h_scr) = refs
    j = pl.program_id(2)

    @pl.when(j == 0)
    def _():
        x = x_ref[0]
        if has_ff:
            x = x + g2_ref[0] * ff_ref[0]
            xo_ref[0] = x
        ms = jnp.mean(x * x, axis=-1, keepdims=True)
        h = x * lax.rsqrt(ms + EPS) * nw_ref[...]
        h = h * (1.0 + sc_ref[0]) + sh_ref[0]
        h16 = h.astype(BF16)
        h_scr[...] = h16
        if_ref[0] = _dot(h16, wif_ref[...]) + bif_ref[...]

    u = _dot(h_scr[...], w_ref[...])
    u_ref[0] = u.astype(BF16)

    @pl.when((j >= J_SK) & (j < J_SK + 2))
    def _():
        kv_ref[0] = u


def _in_proj(x, ff, g2, sh, sc, nw, w16, wif16, bif, tl):
    B, L, D = x.shape
    lm = sh.shape[1]
    has_ff = ff is not None
    tlm = 1 if lm == 1 else tl
    row = lambda b, i, j: (b, i, 0)
    mod = (lambda b, i, j: (b, 0, 0)) if lm == 1 else row
    fixed = lambda b, i, j: (0, 0)
    in_specs = [pl.BlockSpec((1, tl, D), row)]
    args = [x]
    if has_ff:
        in_specs += [pl.BlockSpec((1, tl, D), row), pl.BlockSpec((1, tlm, D), mod)]
        args += [ff, g2]
    in_specs += [
        pl.BlockSpec((1, tlm, D), mod),
        pl.BlockSpec((1, tlm, D), mod),
        pl.BlockSpec((1, D), fixed),
        pl.BlockSpec((D, TN_IN), lambda b, i, j: (0, j)),
        pl.BlockSpec((D, LANES), fixed),
        pl.BlockSpec((1, LANES), fixed),
    ]
    args += [sh, sc, nw, w16, wif16, bif]
    out_specs = [
        pl.BlockSpec((1, tl, TN_IN), lambda b, i, j: (b, i, j)),
        pl.BlockSpec((1, tl, W_SB), lambda b, i, j: (b, i, jnp.clip(j - J_SK, 0, 1))),
        pl.BlockSpec((1, tl, LANES), row),
    ]
    out_shape = [
        jax.ShapeDtypeStruct((B, L, U_W), BF16),
        jax.ShapeDtypeStruct((B, L, 2 * W_SB), F32),
        jax.ShapeDtypeStruct((B, L, LANES), F32),
    ]
    if has_ff:
        out_specs.append(pl.BlockSpec((1, tl, D), row))
        out_shape.append(jax.ShapeDtypeStruct((B, L, D), F32))
    return pl.pallas_call(
        functools.partial(_in_proj_kernel, has_ff),
        grid=(B, L // tl, U_W // TN_IN),
        in_specs=in_specs,
        out_specs=out_specs,
        out_shape=out_shape,
        scratch_shapes=[pltpu.VMEM((tl, D), BF16)],
        compiler_params=_cparams(("arbitrary", "arbitrary", "arbitrary")),
    )(*args)


def _mlstm_kernel(valid, q_ref, k_ref, v_ref, o_ref, if_ref, c0_ref, n0_ref, m0_ref, nw_ref,
                  h_ref, co_ref, no_ref, mo_ref, c_scr, n_scr, m_scr, gt_scr):
    hd = pl.program_id(1)
    c = pl.program_id(2)
    cs = M_CHUNK

    @pl.when(c == 0)
    def _():
        c_scr[...] = c0_ref[0, 0]
        n_scr[...] = n0_ref[0, 0]
        m_scr[...] = m0_ref[0, 0]

    q = q_ref[0]
    ksc = k_ref[0].astype(F32) * (DH_M ** -0.5)
    kb = ksc.astype(BF16)
    v = v_ref[0]
    gates = if_ref[0]
    gt_scr[...] = gates.T
    lane = lax.broadcasted_iota(jnp.int32, (cs, LANES), 1)
    i_col = jnp.sum(jnp.where(lane == hd, gates, 0.0), axis=1, keepdims=True)
    f_col = jnp.sum(jnp.where(lane == hd + NH_M, gates, 0.0), axis=1, keepdims=True)
    i_row = gt_scr[pl.ds(hd, 1), :]
    f_row = gt_scr[pl.ds(hd + NH_M, 1), :]
    lf_col = _log_sigmoid(f_col)
    lf_row = _log_sigmoid(f_row)
    t_idx = lax.broadcasted_iota(jnp.int32, (cs, cs), 0)
    s_idx = lax.broadcasted_iota(jnp.int32, (cs, cs), 1)
    if valid < cs:
        col_ok = t_idx[:, :1] < valid
        row_ok = s_idx[:1, :] < valid
        i_col = jnp.where(col_ok, i_col, NEG_BIG)
        lf_col = jnp.where(col_ok, lf_col, 0.0)
        i_row = jnp.where(row_ok, i_row, NEG_BIG)
        lf_row = jnp.where(row_ok, lf_row, 0.0)
    causal = s_idx <= t_idx
    F_col = jnp.sum(jnp.where(causal, lf_row, 0.0), axis=1, keepdims=True)
    F_row = jnp.sum(jnp.where(t_idx <= s_idx, lf_col, 0.0), axis=0, keepdims=True)
    F_end = jnp.sum(lf_row, axis=1, keepdims=True)
    m_prev = m_scr[:, 0:1]

    dlog = jnp.where(causal, F_col - F_row + i_row, -jnp.inf)
    b_inter = F_col + m_prev
    m_t = jnp.maximum(b_inter, jnp.max(dlog, axis=1, keepdims=True))
    s = _dot_nt(q, kb) * jnp.exp(dlog - m_t)
    a_inter = jnp.exp(b_inter - m_t)
    Cmat = c_scr[...]
    nvec = n_scr[...]
    num = _dot(s.astype(BF16), v) + a_inter * _dot_nt(q, Cmat.astype(BF16))
    qn = jnp.sum(q.astype(F32) * nvec, axis=1, keepdims=True)
    den = jnp.sum(s, axis=1, keepdims=True) + a_inter * qn
    den = jnp.maximum(jnp.abs(den), jnp.exp(-m_t))
    h = num / den

    dec_row = F_end - F_row + i_row
    dec_col = F_end - F_col + i_col
    m_new = jnp.maximum(F_end + m_prev, jnp.max(dec_row, axis=1, keepdims=True))
    w_col = jnp.exp(dec_col - m_new)
    keep = jnp.exp(F_end + m_prev - m_new)
    c_scr[...] = keep * Cmat + _dot_tn((w_col * v.astype(F32)).astype(BF16), kb)
    n_scr[...] = keep * nvec + jnp.sum(w_col * ksc, axis=0, keepdims=True)
    m_scr[...] = jnp.broadcast_to(m_new, m_scr.shape)

    mu = jnp.mean(h, axis=-1, keepdims=True)
    var = jnp.mean(jnp.square(h - mu), axis=-1, keepdims=True)
    hn = (h - mu) * lax.rsqrt(var + EPS) * nw_ref[0]
    h_ref[0] = (hn * _sigmoid(o_ref[0].astype(F32))).astype(BF16)

    @pl.when(c == pl.num_programs(2) - 1)
    def _():
        co_ref[0, 0] = c_scr[...]
        no_ref[0, 0] = n_scr[...]
        mo_ref[0, 0] = m_scr[...]


def _mlstm(um, if32, C0, n0, m0, mnorm_w, valid):
    B, L = um.shape[0], um.shape[1]
    nc = L // M_CHUNK
    blk = lambda off: pl.BlockSpec((1, M_CHUNK, DH_M), lambda b, h, c: (b, c, off + h))
    st = lambda r: pl.BlockSpec((1, 1, r, DH_M), lambda b, h, c: (b, h, 0, 0))
    return pl.pallas_call(
        functools.partial(_mlstm_kernel, valid),
        grid=(B, NH_M, nc),
        in_specs=[
            blk(0), blk(NH_M), blk(2 * NH_M), blk(3 * NH_M),
            pl.BlockSpec((1, M_CHUNK, LANES), lambda b, h, c: (b, c, 0)),
            st(DH_M), st(1), st(1),
            pl.BlockSpec((1, 1, DH_M), lambda b, h, c: (h, 0, 0)),
        ],
        out_specs=[
            pl.BlockSpec((1, M_CHUNK, DH_M), lambda b, h, c: (b, c, h)),
            st(DH_M), st(1), st(1),
        ],
        out_shape=[
            jax.ShapeDtypeStruct((B, L, W_M), BF16),
            jax.ShapeDtypeStruct((B, NH_M, DH_M, DH_M), F32),
            jax.ShapeDtypeStruct((B, NH_M, 1, DH_M), F32),
            jax.ShapeDtypeStruct((B, NH_M, 1, DH_M), F32),
        ],
        scratch_shapes=[
            pltpu.VMEM((DH_M, DH_M), F32),
            pltpu.VMEM((1, DH_M), F32),
            pltpu.VMEM((1, DH_M), F32),
            pltpu.VMEM((LANES, M_CHUNK), F32),
        ],
        compiler_params=_cparams(("arbitrary", "arbitrary", "arbitrary")),
    )(um, um, um, um, if32, C0, n0.reshape(B, NH_M, 1, DH_M),
      jnp.broadcast_to(m0[:, :, None, None], (B, NH_M, 1, DH_M)),
      mnorm_w.reshape(NH_M, 1, DH_M))


def _conv_kernel(tl, rb, a_ref, g_ref, st_ref, w_ref, b_ref, lw_ref, lb_ref,
                 o_ref, so_ref, buf, shifted):
    i = pl.program_id(1)
    sub = 8

    @pl.when(i == 0)
    def _():
        buf[0:CONV_HIST, :] = st_ref[0]

    a = a_ref[0].astype(F32) * _sigmoid(g_ref[0].astype(F32))
    buf[CONV_HIST:CONV_HIST + tl, :] = a
    pad = CONV_HIST - (CONV_W - 1)
    span = shifted.shape[1]
    for s in range(1, sub):
        shifted[s - 1] = buf[s:s + span, :]

    def block(r, carry):
        base = pl.multiple_of(r * rb, rb)
        acc = jnp.broadcast_to(b_ref[...], (rb, CONV_CH))
        for j in range(CONV_W):
            qq, s = divmod(j + pad, sub)
            start = pl.multiple_of(base + sub * qq, sub)
            if s == 0:
                rows = buf[pl.ds(start, rb), :]
            else:
                rows = shifted[s - 1, pl.ds(start, rb), :]
            acc = acc + w_ref[j:j + 1, :] * rows
        mu = jnp.mean(acc, axis=-1, keepdims=True)
        var = jnp.mean(jnp.square(acc - mu), axis=-1, keepdims=True)
        y = (acc - mu) * lax.rsqrt(var + EPS) * lw_ref[...] + lb_ref[...]
        o_ref[0, pl.ds(base, rb), :] = (y * _sigmoid(y)).astype(o_ref.dtype)
        return carry

    lax.fori_loop(0, tl // rb, block, 0)
    tail = buf[tl:tl + CONV_HIST, :]
    buf[0:CONV_HIST, :] = tail

    @pl.when(i == pl.num_programs(1) - 1)
    def _():
        so_ref[0] = tail


def _conv(ua, ug, off_a, off_g, state, dw_w, dw_b, ln_w, ln_b, tl):
    B, L = ua.shape[0], ua.shape[1]
    rb = min(tl, 32)
    pad = CONV_HIST - (CONV_W - 1)
    st = jnp.pad(state, ((0, 0), (pad, 0), (0, 0)))
    w = jnp.pad(dw_w, ((0, CONV_HIST - CONV_W), (0, 0)))
    vec = lambda b, i: (0, 0)
    out, so = pl.pallas_call(
        functools.partial(_conv_kernel, tl, rb),
        grid=(B, L // tl),
        in_specs=[
            pl.BlockSpec((1, tl, CONV_CH), lambda b, i: (b, i, off_a)),
            pl.BlockSpec((1, tl, CONV_CH), lambda b, i: (b, i, off_g)),
            pl.BlockSpec((1, CONV_HIST, CONV_CH), lambda b, i: (b, 0, 0)),
            pl.BlockSpec((CONV_HIST, CONV_CH), vec),
            pl.BlockSpec((1, CONV_CH), vec),
            pl.BlockSpec((1, CONV_CH), vec),
            pl.BlockSpec((1, CONV_CH), vec),
        ],
        out_specs=[
            pl.BlockSpec((1, tl, CONV_CH), lambda b, i: (b, i, 0)),
            pl.BlockSpec((1, CONV_HIST, CONV_CH), lambda b, i: (b, 0, 0)),
        ],
        out_shape=[
            jax.ShapeDtypeStruct((B, L, CONV_CH), ua.dtype),
            jax.ShapeDtypeStruct((B, CONV_HIST, CONV_CH), F32),
        ],
        scratch_shapes=[pltpu.VMEM((CONV_HIST + tl, CONV_CH), F32),
                        pltpu.VMEM((7, CONV_HIST + tl - 8, CONV_CH), F32)],
        compiler_params=_cparams(("arbitrary", "arbitrary")),
    )(ua, ug, st, w, dw_b.reshape(1, -1), ln_w.reshape(1, -1), ln_b.reshape(1, -1))
    return out, so[:, pad:, :]


def _sb_tri(tk):
    r = lax.broadcasted_iota(jnp.int32, (tk, tk + LANES), 0)
    c = lax.broadcasted_iota(jnp.int32, (tk, tk + LANES), 1)
    return jnp.where((c >= tk) | (r > c), 1.0, 0.0).astype(BF16)


SB_HEADS = 4
SB_W = SB_HEADS * DH_SB
SB_TQ = 256
SB_CB = 256


def _sb_prompt_kernel(tq, tk, bias_ref, tri_ref, q_ref, k_ref, v_ref, o_ref, acc_scr, car_scr):
    p = pl.program_id(1)
    qi = pl.program_id(2)
    cb = min(SB_CB, tk)
    nb = tk // cb
    nsub = tq // LANES
    ng = nsub * SB_HEADS
    lane = lax.broadcasted_iota(jnp.int32, (1, SB_W), 1)
    q2 = q_ref[0].astype(F32) * (DH_SB ** -0.5)
    qs = jnp.concatenate(
        [jnp.where((lane // DH_SB) == hh, q2[r * LANES:(r + 1) * LANES], 0.0).astype(BF16)
         for r in range(nsub) for hh in range(SB_HEADS)], axis=0)
    bias = [bias_ref[SB_HEADS * p + hh] for hh in range(SB_HEADS)]
    acc_scr[...] = jnp.zeros_like(acc_scr)
    car_scr[...] = jnp.zeros_like(car_scr)
    rr = lax.broadcasted_iota(jnp.int32, (LANES, tk), 0)
    cc = lax.broadcasted_iota(jnp.int32, (LANES, tk), 1)
    rows = lambda a, i: a[i * LANES:(i + 1) * LANES]

    def tile(kt, masked):
        start = pl.multiple_of(kt * tk, tk)
        ktile = k_ref[0, pl.ds(start, tk), :]
        vtile = v_ref[0, pl.ds(start, tk), :]
        if masked:
            visible = [(cc + kt * tk) < (rr + (qi * tq + r * LANES)) for r in range(nsub)]
        z_all = _dot_nt(qs, ktile)
        log_beta, log_1m = [], []
        for g in range(ng):
            z = rows(z_all, g) + bias[g % SB_HEADS]
            sp = jnp.log(1.0 + jnp.exp(-jnp.abs(z)))
            lb = jnp.minimum(z, 0.0) - sp
            l1 = lb - z
            if masked:
                l1 = jnp.where(visible[g // SB_HEADS], l1, 0.0)
            log_beta.append(lb)
            log_1m.append(l1)
        stack = jnp.concatenate([log_1m[g][:, j * cb:(j + 1) * cb].astype(BF16)
                                 for j in range(nb) for g in range(ng)], axis=0)
        later_all = _dot(stack, tri_ref[...])
        a_rows = []
        for g in range(ng):
            car = car_scr[g]
            parts = [None] * nb
            for j in reversed(range(nb)):
                sl = slice(j * cb, (j + 1) * cb)
                later = rows(later_all, j * ng + g)
                parts[j] = jnp.exp(log_beta[g][:, sl] + later + jnp.tile(car, (1, cb // LANES)))
                car = car + jnp.sum(log_1m[g][:, sl], axis=1, keepdims=True)
            car_scr[g] = car
            A = jnp.concatenate(parts, axis=1)
            if masked:
                A = jnp.where(visible[g // SB_HEADS], A, 0.0)
            a_rows.append(A.astype(BF16))
        acc_scr[...] = acc_scr[...] + _dot(jnp.concatenate(a_rows, axis=0), vtile)

    kt_d = (qi * tq) // tk
    tile(kt_d, True)

    def body(it, carry):
        tile(kt_d - 1 - it, False)
        return carry

    lax.fori_loop(0, kt_d, body, 0)
    head = lax.broadcasted_iota(jnp.int32, (LANES, SB_W), 1) // DH_SB
    for r in range(nsub):
        out = acc_scr[(r * SB_HEADS) * LANES:(r * SB_HEADS + 1) * LANES]
        for hh in range(1, SB_HEADS):
            g = r * SB_HEADS + hh
            out = jnp.where(head == hh, acc_scr[g * LANES:(g + 1) * LANES], out)
        o_ref[0, r * LANES:(r + 1) * LANES, :] = out.astype(BF16)


def _sb_prompt(u16, sb_b):
    B, L = u16.shape[0], u16.shape[1]
    tq = min(SB_TQ, L)
    tk = min(512, L)
    cb = min(SB_CB, tk)
    c0 = U_S // SB_W
    nq = W_SB // SB_W
    r = lax.broadcasted_iota(jnp.int32, (cb, cb), 0)
    c = lax.broadcasted_iota(jnp.int32, (cb, cb), 1)
    tri = jnp.where(r > c, 1.0, 0.0).astype(BF16)
    ng = (tq // LANES) * SB_HEADS
    return pl.pallas_call(
        functools.partial(_sb_prompt_kernel, tq, tk),
        grid=(B, nq, L // tq),
        in_specs=[
            pl.BlockSpec(memory_space=pltpu.SMEM),
            pl.BlockSpec((cb, cb), lambda b, p, i: (0, 0)),
            pl.BlockSpec((1, tq, SB_W), lambda b, p, i: (b, i, c0 + p)),
            pl.BlockSpec((1, L, SB_W), lambda b, p, i: (b, 0, c0 + nq + p)),
            pl.BlockSpec((1, L, SB_W), lambda b, p, i: (b, 0, c0 + 2 * nq + p)),
        ],
        out_specs=pl.BlockSpec((1, tq, SB_W), lambda b, p, i: (b, i, p)),
        out_shape=jax.ShapeDtypeStruct((B, L, W_SB), BF16),
        scratch_shapes=[pltpu.VMEM((ng * LANES, SB_W), F32),
                        pltpu.VMEM((ng, LANES, LANES), F32)],
        compiler_params=_cparams(("arbitrary", "arbitrary", "arbitrary")),
    )(sb_b, tri, u16, u16, u16)


PAGES_PER_STEP = 8


def _sb_sample_kernel(lq, pt_ref, qbd_ref, bias_ref, tri_ref, kn_ref, vn_ref, *refs):
    g = PAGES_PER_STEP
    k_refs = refs[:g]
    v_refs = refs[g:2 * g]
    o_ref = refs[2 * g]
    acc_scr, car_scr = refs[2 * g + 1:]
    s = pl.program_id(1)
    nr = NH_SB * lq
    qbd = qbd_ref[0]
    bias = bias_ref[...]
    tri = tri_ref[...]

    def scores(k16, visible):
        z = _dot_nt(qbd, k16) + jnp.tile(bias, (1, k16.shape[0] // LANES))
        sp = jnp.log(1.0 + jnp.exp(-jnp.abs(z)))
        log_beta = jnp.minimum(z, 0.0) - sp
        log_1m = log_beta - z
        if visible is not None:
            log_1m = jnp.where(visible, log_1m, 0.0)
        return log_beta, log_1m

    def weights(log_beta, log_1m, car):
        hi, lo = _split_bf16(log_1m)
        parts = []
        for j in range(log_1m.shape[1] // LANES):
            sl = slice(j * LANES, (j + 1) * LANES)
            cum = _dot(hi[:, sl], tri) + _dot(lo[:, sl], tri)
            parts.append(jnp.exp(log_beta[:, sl] + cum[:, :LANES] + car))
            car = car + cum[:, LANES:]
        return parts, car

    @pl.when(s == 0)
    def _():
        r = lax.broadcasted_iota(jnp.int32, (nr, PAGE_SIZE), 0)
        c = lax.broadcasted_iota(jnp.int32, (nr, PAGE_SIZE), 1)
        visible = c < (r % lq)
        log_beta, log_1m = scores(kn_ref[0].astype(BF16), visible)
        parts, car = weights(log_beta, log_1m, jnp.zeros((nr, LANES), F32))
        A = jnp.where(visible, parts[0], 0.0)
        acc_scr[...] = _dot(A.astype(BF16), vn_ref[0].astype(BF16))
        car_scr[...] = car

    k16 = jnp.concatenate([k_refs[gi][0].astype(BF16) for gi in range(g)], axis=0)
    v16 = jnp.concatenate([v_refs[gi][0].astype(BF16) for gi in range(g)], axis=0)
    log_beta, log_1m = scores(k16, None)
    parts, car = weights(log_beta, log_1m, car_scr[...])
    A = jnp.concatenate(parts, axis=1).astype(BF16)
    acc = acc_scr[...] + _dot(A, v16)
    acc_scr[...] = acc
    car_scr[...] = car

    @pl.when(s == pl.num_programs(1) - 1)
    def _():
        r = lax.broadcasted_iota(jnp.int32, (nr, W_SB), 0)
        c = lax.broadcasted_iota(jnp.int32, (nr, W_SB), 1)
        m = jnp.where((r // lq) == (c // DH_SB), acc, 0.0)
        out = m[0:lq]
        for hh in range(1, NH_SB):
            out = out + m[hh * lq:(hh + 1) * lq]
        o_ref[0] = out


def _sb_sample(q16, k_new, v_new, cache_k, cache_v, page_table, sb_b):
    B, lq, _ = q16.shape
    n_pages = page_table.shape[1]
    g = PAGES_PER_STEP
    nr = NH_SB * lq
    k_new = jnp.pad(k_new, ((0, 0), (0, PAGE_SIZE - lq), (0, 0)))
    v_new = jnp.pad(v_new, ((0, 0), (0, PAGE_SIZE - lq), (0, 0)))
    qh = (q16.astype(F32) * (DH_SB ** -0.5)).reshape(B, lq, NH_SB, DH_SB)
    qbd = jnp.einsum('bihd,hg->bhigd', qh, jnp.eye(NH_SB, dtype=F32)).reshape(B, nr, W_SB).astype(BF16)
    bias = jnp.broadcast_to(jnp.repeat(sb_b.astype(F32), lq)[:, None], (nr, LANES))

    def page_spec(gi):
        return pl.BlockSpec((1, PAGE_SIZE, W_SB),
                            lambda b, s, pt: (pt[b, n_pages - 1 - (s * g + gi)], 0, 0))

    fixed = lambda b, s, pt: (0, 0)
    seq = lambda b, s, pt: (b, 0, 0)
    grid_spec = pltpu.PrefetchScalarGridSpec(
        num_scalar_prefetch=1,
        grid=(B, n_pages // g),
        in_specs=[
            pl.BlockSpec((1, nr, W_SB), seq),
            pl.BlockSpec((nr, LANES), fixed),
            pl.BlockSpec((PAGE_SIZE, PAGE_SIZE + LANES), fixed),
            pl.BlockSpec((1, PAGE_SIZE, W_SB), seq),
            pl.BlockSpec((1, PAGE_SIZE, W_SB), seq),
        ] + [page_spec(gi) for gi in range(g)] * 2,
        out_specs=pl.BlockSpec((1, lq, W_SB), seq),
        scratch_shapes=[pltpu.VMEM((nr, W_SB), F32), pltpu.VMEM((nr, LANES), F32)],
    )
    return pl.pallas_call(
        functools.partial(_sb_sample_kernel, lq),
        grid_spec=grid_spec,
        out_shape=jax.ShapeDtypeStruct((B, lq, W_SB), F32),
        compiler_params=_cparams(("arbitrary", "arbitrary")),
    )(page_table, qbd, bias, _sb_tri(PAGE_SIZE), k_new, v_new, *([cache_k] * g), *([cache_v] * g))


def _merge_kernel(hm_ref, hc_ref, ha_ref, gm_ref, gc_ref, ga_ref, x_ref, g1_ref, sh_ref, sc_ref,
                  wpm_ref, wpc_ref, wpa_ref, wo_ref, nw_ref, rwh_ref, rwl_ref, rb_ref,
                  xo_ref, h2_ref, gate_ref):
    y = _sigmoid(gm_ref[0].astype(F32)) * _dot(hm_ref[0], wpm_ref[...])
    y = y + _sigmoid(gc_ref[0].astype(F32)) * _dot(hc_ref[0], wpc_ref[...])
    y = y + _sigmoid(ga_ref[0].astype(F32)) * _dot(ha_ref[0], wpa_ref[...])
    mix = _dot(y.astype(BF16), wo_ref[...])
    x = x_ref[0] + g1_ref[0] * mix
    xo_ref[0] = x
    ms = jnp.mean(x * x, axis=-1, keepdims=True)
    h = x * lax.rsqrt(ms + EPS) * nw_ref[...]
    h = h * (1.0 + sc_ref[0]) + sh_ref[0]
    h_hi, h_lo = _split_bf16(h)
    h2_ref[0] = h_hi
    logits = (_dot(h_hi, rwh_ref[...]) + _dot(h_lo, rwh_ref[...]) + _dot(h_hi, rwl_ref[...])
              + rb_ref[...])
    lane = lax.broadcasted_iota(jnp.int32, logits.shape, 1).astype(F32)
    work = logits
    sel = jnp.zeros(logits.shape, F32)
    vmax = None
    for _ in range(TOP_K):
        mx = jnp.max(work, axis=-1, keepdims=True)
        if vmax is None:
            vmax = mx
        first = jnp.min(jnp.where(work == mx, lane, float(LANES)), axis=-1, keepdims=True)
        pick = lane == first
        sel = jnp.where(pick, 1.0, sel)
        work = jnp.where(pick, -jnp.inf, work)
    e = sel * jnp.exp(logits - vmax)
    gate_ref[0] = e / jnp.sum(e, axis=-1, keepdims=True)


def _merge(hm, hc, ha, u16, x, g1, sh2, sc2, wpm, wpc, wpa, wo, nw, rw_hi, rw_lo, rb, tl):
    B, L, D = x.shape
    lm = g1.shape[1]
    tlm = 1 if lm == 1 else tl
    row = lambda b, i: (b, i, 0)
    mod = (lambda b, i: (b, 0, 0)) if lm == 1 else row
    fixed = lambda b, i: (0, 0)
    gb = U_G // D
    half = lambda: pl.BlockSpec((1, tl, W_M), row)
    gate = lambda n: pl.BlockSpec((1, tl, D), lambda b, i: (b, i, gb + n))
    modspec = lambda: pl.BlockSpec((1, tlm, D), mod)
    return pl.pallas_call(
        _merge_kernel,
        grid=(B, L // tl),
        in_specs=[
            half(), half(), half(), gate(0), gate(1), gate(2),
            pl.BlockSpec((1, tl, D), row), modspec(), modspec(), modspec(),
            pl.BlockSpec((W_M, D), fixed), pl.BlockSpec((CONV_CH, D), fixed),
            pl.BlockSpec((W_SB, D), fixed), pl.BlockSpec((D, D), fixed),
            pl.BlockSpec((1, D), fixed),
            pl.BlockSpec((D, LANES), fixed), pl.BlockSpec((D, LANES), fixed),
            pl.BlockSpec((1, LANES), fixed),
        ],
        out_specs=[
            pl.BlockSpec((1, tl, D), row),
            pl.BlockSpec((1, tl, D), row),
            pl.BlockSpec((1, tl, LANES), row),
        ],
        out_shape=[
            jax.ShapeDtypeStruct((B, L, D), F32),
            jax.ShapeDtypeStruct((B, L, D), BF16),
            jax.ShapeDtypeStruct((B, L, LANES), F32),
        ],
        compiler_params=_cparams(("arbitrary", "arbitrary")),
    )(hm, hc, ha, u16, u16, u16, x, g1, sh2, sc2, wpm, wpc, wpa, wo, nw, rw_hi, rw_lo, rb)


def _moe_kernel(h_ref, gate_ref, w1_ref, b1_ref, w2_ref, b2_ref, o_ref):
    e = pl.program_id(1)

    @pl.when(e == 0)
    def _():
        o_ref[...] = jnp.zeros_like(o_ref)

    gates = gate_ref[...]
    lane = lax.broadcasted_iota(jnp.int32, gates.shape, 1)
    ge = jnp.sum(jnp.where(lane == e, gates, 0.0), axis=-1, keepdims=True)
    z = _dot(h_ref[...], w1_ref[0]) + b1_ref[0]
    z_glu = jnp.minimum(z[:, :D_FF], SWIGLU_LIMIT)
    z_lin = jnp.clip(z[:, D_FF:], -SWIGLU_LIMIT, SWIGLU_LIMIT)
    act = z_glu * _sigmoid(SWIGLU_ALPHA * z_glu) * (z_lin + 1.0)
    y = _dot(act.astype(BF16), w2_ref[0]) + b2_ref[0]
    o_ref[...] += ge * y


def _moe(h2, gates, w1, b1, w2, b2, tm):
    N, D = h2.shape
    return pl.pallas_call(
        _moe_kernel,
        grid=(N // tm, N_EXP),
        in_specs=[
            pl.BlockSpec((tm, D), lambda i, e: (i, 0)),
            pl.BlockSpec((tm, LANES), lambda i, e: (i, 0)),
            pl.BlockSpec((1, D, 2 * D_FF), lambda i, e: (e, 0, 0)),
            pl.BlockSpec((1, 1, 2 * D_FF), lambda i, e: (e, 0, 0)),
            pl.BlockSpec((1, D_FF, D), lambda i, e: (e, 0, 0)),
            pl.BlockSpec((1, 1, D), lambda i, e: (e, 0, 0)),
        ],
        out_specs=pl.BlockSpec((tm, D), lambda i, e: (i, 0)),
        out_shape=jax.ShapeDtypeStruct((N, D), F32),
        compiler_params=_cparams(("arbitrary", "arbitrary")),
    )(h2, gates, w1, b1.reshape(N_EXP, 1, 2 * D_FF), w2, b2.reshape(N_EXP, 1, D))


def _final_kernel(x_ref, ff_ref, g2_ref, w_ref, o_ref):
    x = x_ref[0] + g2_ref[0] * ff_ref[0]
    ms = jnp.mean(x * x, axis=-1, keepdims=True)
    o_ref[0] = x * lax.rsqrt(ms + EPS) * w_ref[...]


def _final(x, ff, g2, w, tl):
    B, L, D = x.shape
    lm = g2.shape[1]
    tlm = 1 if lm == 1 else tl
    row = lambda b, i: (b, i, 0)
    mod = (lambda b, i: (b, 0, 0)) if lm == 1 else row
    return pl.pallas_call(
        _final_kernel,
        grid=(B, L // tl),
        in_specs=[
            pl.BlockSpec((1, tl, D), row), pl.BlockSpec((1, tl, D), row),
            pl.BlockSpec((1, tlm, D), mod), pl.BlockSpec((1, D), lambda b, i: (0, 0)),
        ],
        out_specs=pl.BlockSpec((1, tl, D), row),
        out_shape=jax.ShapeDtypeStruct((B, L, D), F32),
        compiler_params=_cparams(("arbitrary", "arbitrary")),
    )(x, ff, g2, w.reshape(1, D))


def _prep_layer_weights(l, w_in, mlstm_gate_b, router_w, router_b):
    wl = w_in[l]
    w_main = jnp.concatenate(
        [wl[:, OFF_MQ:OFF_MIF], wl[:, OFF_CG:OFF_SQ], wl[:, OFF_G:], wl[:, OFF_SQ:OFF_G]], axis=1)
    wif = jnp.pad(wl[:, OFF_MIF:OFF_CG], ((0, 0), (0, LANES - 2 * NH_M)))
    bif = jnp.pad(mlstm_gate_b[l], (0, LANES - 2 * NH_M)).reshape(1, LANES)
    rw = jnp.pad(router_w[l], ((0, 0), (0, LANES - N_EXP)))
    rw_hi = rw.astype(BF16)
    rw_lo = (rw - rw_hi.astype(F32)).astype(BF16)
    rb = jnp.pad(router_b[l], (0, LANES - N_EXP), constant_values=NEG_BIG).reshape(1, LANES)
    return w_main.astype(BF16), wif.astype(BF16), bif, rw_hi, rw_lo, rb


def _trunk(x, mods, C0, n0, m0, conv0, paged, W, tl, tl_mg, tm):
    B, L, D = x.shape
    sh1, sc1, g1, sh2, sc2, g2 = mods
    ks, vs, Cs, ns, ms, bufs = [], [], [], [], [], []
    ff = None
    for l in range(DEPTH):
        w16, wif16, bif, rw_hi, rw_lo, rb = W['prep'][l]
        nw1 = W['norm1_w'][l].reshape(1, D)
        if ff is None:
            u16, kv32, if32 = _in_proj(x, None, None, sh1[l], sc1[l], nw1, w16, wif16, bif, tl)
        else:
            u16, kv32, if32, x = _in_proj(x, ff, g2[l - 1], sh1[l], sc1[l], nw1, w16, wif16, bif, tl)
        if paged is None:
            hm, Cn, nn, mn = _mlstm(u16, if32, C0[l], n0[l], m0[l], W['mlstm_norm_w'][l], M_CHUNK)
            hc, buf = _conv(u16, u16, U_CG // CONV_CH, U_CG // CONV_CH + 1, conv0[l],
                            W['conv_dw_w'][l], W['conv_dw_b'][l], W['conv_ln_w'][l], W['conv_ln_b'][l],
                            min(512, L))
            ha = _sb_prompt(u16, W['sb_bias'][l])
            k_new = kv32[..., :W_SB].reshape(B, L, NH_SB, DH_SB)
            v_new = kv32[..., W_SB:].reshape(B, L, NH_SB, DH_SB)
        else:
            cache_k, cache_v, page_table, bs, ls = paged
            us = u16.reshape(bs, ls, U_W)
            padl = M_CHUNK - ls
            um = jnp.pad(us[..., :4 * W_M], ((0, 0), (0, padl), (0, 0)))
            ifp = jnp.pad(if32.reshape(bs, ls, LANES), ((0, 0), (0, padl), (0, 0)))
            hm, Cn, nn, mn = _mlstm(um, ifp, C0[l], n0[l], m0[l], W['mlstm_norm_w'][l], ls)
            hm = hm[:, :ls].reshape(B, L, W_M)
            ucg = us[..., U_CG:U_CG + 2 * CONV_CH].astype(F32)
            hc, buf = _conv(ucg, ucg, 0, 1, conv0[l],
                            W['conv_dw_w'][l], W['conv_dw_b'][l], W['conv_ln_w'][l], W['conv_ln_b'][l], ls)
            hc = hc.reshape(B, L, CONV_CH).astype(BF16)
            kvs = kv32.reshape(bs, ls, 2 * W_SB)
            ha = _sb_sample(us[..., U_S:U_S + W_SB], kvs[..., :W_SB], kvs[..., W_SB:],
                            cache_k[l].reshape(-1, PAGE_SIZE, W_SB), cache_v[l].reshape(-1, PAGE_SIZE, W_SB),
                            page_table, W['sb_bias'][l]).reshape(B, L, W_SB).astype(BF16)
            k_new = kvs[..., :W_SB].reshape(bs, ls, NH_SB, DH_SB)
            v_new = kvs[..., W_SB:].reshape(bs, ls, NH_SB, DH_SB)
        x, h2, gates = _merge(hm, hc, ha, u16, x, g1[l], sh2[l], sc2[l],
                              W['w_pm16'][l], W['w_pc16'][l], W['w_pa16'][l], W['w_out16'][l],
                              W['norm2_w'][l].reshape(1, D), rw_hi, rw_lo, rb, tl_mg)
        ff = _moe(h2.reshape(B * L, D), gates.reshape(B * L, LANES),
                  W['exp_w1_16'][l], W['exp_b1'][l], W['exp_w2_16'][l], W['exp_b2'][l], tm).reshape(B, L, D)
        ks.append(k_new)
        vs.append(v_new)
        Cs.append(Cn)
        ns.append(nn[:, :, 0, :])
        ms.append(mn[:, :, 0, 0])
        bufs.append(buf)
    y = _final(x, ff, g2[DEPTH - 1], W['final_w'], tl)
    return y, jnp.stack(ks), jnp.stack(vs), jnp.stack(Cs), jnp.stack(ns), jnp.stack(ms), jnp.stack(bufs)


def kernel(x_prompt, x_sample, c_prompt, c_sample, cache_k, cache_v, state_C, state_n, state_m, state_conv, page_table, w_ada, b_ada, norm1_w, norm2_w, w_in, mlstm_gate_b, mlstm_norm_w, w_pm, conv_dw_w, conv_dw_b, conv_ln_w, conv_ln_b, w_pc, w_pa, sb_bias, w_out, router_w, router_b, exp_w1, exp_b1, exp_w2, exp_b2, final_w):
    BP, LP, D = x_prompt.shape
    BS, LS, _ = x_sample.shape
    W = dict(norm1_w=norm1_w, norm2_w=norm2_w, mlstm_norm_w=mlstm_norm_w,
             conv_dw_w=conv_dw_w, conv_dw_b=conv_dw_b, conv_ln_w=conv_ln_w, conv_ln_b=conv_ln_b,
             sb_bias=sb_bias, exp_b1=exp_b1, exp_b2=exp_b2, final_w=final_w,
             w_pm16=w_pm.astype(BF16), w_pc16=w_pc.astype(BF16), w_pa16=w_pa.astype(BF16),
             w_out16=w_out.astype(BF16), exp_w1_16=exp_w1.astype(BF16), exp_w2_16=exp_w2.astype(BF16))
    W['prep'] = [_prep_layer_weights(l, w_in, mlstm_gate_b, router_w, router_b) for l in range(DEPTH)]

    nb = BP + BS
    nbp = -(-nb // 8) * 8
    c_all = jnp.pad(jnp.concatenate([c_prompt, c_sample], axis=0), ((0, nbp - nb), (0, 0)))
    mod = _adaln(c_all, w_ada, b_ada)
    mods_p = [mod[:, :BP, None, i * D:(i + 1) * D] for i in range(6)]
    mods_s = [jnp.repeat(mod[:, BP:nb, i * D:(i + 1) * D], LS, axis=1)[:, None] for i in range(6)]

    zC = jnp.zeros((DEPTH, BP, NH_M, DH_M, DH_M), F32)
    zn = jnp.zeros((DEPTH, BP, NH_M, DH_M), F32)
    zm = jnp.zeros((DEPTH, BP, NH_M), F32)
    zconv = jnp.zeros((DEPTH, BP, CONV_W - 1, CONV_CH), F32)
    yp, *outs_p = _trunk(x_prompt, mods_p, zC, zn, zm, zconv, None, W,
                         min(1024, LP), min(512, LP), min(1024, BP * LP))
    ys, *outs_s = _trunk(x_sample.reshape(1, BS * LS, D), mods_s, state_C, state_n, state_m, state_conv,
                         (cache_k, cache_v, page_table, BS, LS), W, BS * LS, BS * LS, BS * LS)
    return (yp, ys.reshape(BS, LS, D)) + tuple(outs_p) + tuple(outs_s)
```

```python
import functools

import jax
import jax.numpy as jnp
from jax import lax
from jax.experimental import pallas as pl
from jax.experimental.pallas import tpu as pltpu

F32 = jnp.float32
BF16 = jnp.bfloat16

D_MODEL = 1024
DEPTH = 4
NH_M = 4
DH_M = 128
W_M = NH_M * DH_M
M_CHUNK = 128
CONV_CH = 512
CONV_W = 31
CONV_HIST = 32
NH_SB = 8
DH_SB = 64
W_SB = NH_SB * DH_SB
PAGE_SIZE = 128
N_EXP = 32
TOP_K = 4
D_FF = D_MODEL
SWIGLU_ALPHA = 1.702
SWIGLU_LIMIT = 7.0
EPS = 1e-5
LANES = 128
NEG_BIG = -1e30

OFF_MQ = 0
OFF_MIF = 4 * W_M
OFF_CG = OFF_MIF + 2 * NH_M
OFF_SQ = OFF_CG + 2 * CONV_CH
OFF_G = OFF_SQ + 3 * W_SB
D_IN = OFF_G + 3 * D_MODEL

U_M = 0
U_CG = 2048
U_G = 3072
U_S = 6144
U_W = 7680
TN_IN = 512
J_SK = (U_S + W_SB) // TN_IN

VMEM_LIMIT = 56 * 1024 * 1024


def _cparams(sem):
    return pltpu.CompilerParams(dimension_semantics=sem, vmem_limit_bytes=VMEM_LIMIT)


def _log_sigmoid(x):
    return jnp.minimum(x, 0.0) - jnp.log(1.0 + jnp.exp(-jnp.abs(x)))


def _sigmoid(x):
    return 1.0 / (1.0 + jnp.exp(-x))


def _dot(a, b):
    return jnp.dot(a, b, preferred_element_type=F32)


def _dot_nt(a, b):
    return lax.dot_general(a, b, (((1,), (1,)), ((), ())), preferred_element_type=F32)


def _dot_tn(a, b):
    return lax.dot_general(a, b, (((0,), (0,)), ((), ())), preferred_element_type=F32)


def _split_bf16(x):
    hi = x.astype(BF16)
    lo = (x - hi.astype(F32)).astype(BF16)
    return hi, lo


def _cast_kernel(x_ref, o_ref):
    o_ref[...] = x_ref[...].astype(BF16)


def _cast_bf16(w, rows):
    flat = w.reshape(-1, w.shape[-1])
    n, c = flat.shape
    out = pl.pallas_call(
        _cast_kernel,
        grid=(n // rows,),
        in_specs=[pl.BlockSpec((rows, c), lambda i: (i, 0))],
        out_specs=pl.BlockSpec((rows, c), lambda i: (i, 0)),
        out_shape=jax.ShapeDtypeStruct((n, c), BF16),
        compiler_params=_cparams(("arbitrary",)),
    )(flat)
    return out.reshape(w.shape)


def _adaln_kernel(c_ref, w_ref, b_ref, o_ref):
    c = c_ref[...]
    s = c * _sigmoid(c)
    s_hi, s_lo = _split_bf16(s)
    w_hi, w_lo = _split_bf16(w_ref[0])
    o_ref[0] = _dot(s_hi, w_hi) + _dot(s_lo, w_hi) + _dot(s_hi, w_lo) + b_ref[0]


def _adaln(c_all, w_ada, b_ada):
    nb = c_all.shape[0]
    tn = 1536
    return pl.pallas_call(
        _adaln_kernel,
        grid=(DEPTH, 6 * D_MODEL // tn),
        in_specs=[
            pl.BlockSpec((nb, D_MODEL), lambda l, j: (0, 0)),
            pl.BlockSpec((1, D_MODEL, tn), lambda l, j: (l, 0, j)),
            pl.BlockSpec((1, 1, tn), lambda l, j: (l, 0, j)),
        ],
        out_specs=pl.BlockSpec((1, nb, tn), lambda l, j: (l, 0, j)),
        out_shape=jax.ShapeDtypeStruct((DEPTH, nb, 6 * D_MODEL), F32),
        compiler_params=_cparams(("arbitrary", "arbitrary")),
    )(c_all, w_ada, b_ada.reshape(DEPTH, 1, 6 * D_MODEL))


def _in_proj_kernel(has_ff, *refs):
    if has_ff:
        (x_ref, ff_ref, g2_ref, sh_ref, sc_ref, nw_ref, w_ref, wif_ref, bif_ref,
         u_ref, kv_ref, if_ref, xo_ref, h_scr) = refs
    else:
        (x_ref, sh_ref, sc_ref, nw_ref, w_ref, wif_ref, bif_ref,
         u_ref, kv_ref, if_ref, h_scr) = refs
    j = pl.program_id(2)

    @pl.when(j == 0)
    def _():
        x = x_ref[0]
        if has_ff:
            x = x + g2_ref[0] * ff_ref[0]
            xo_ref[0] = x
        ms = jnp.mean(x * x, axis=-1, keepdims=True)
        h = x * lax.rsqrt(ms + EPS) * nw_ref[...]
        h = h * (1.0 + sc_ref[0]) + sh_ref[0]
        h16 = h.astype(BF16)
        h_scr[...] = h16
        if_ref[0] = _dot(h16, wif_ref[...]) + bif_ref[...]

    u = _dot(h_scr[...], w_ref[...])
    u_ref[0] = u.astype(BF16)

    @pl.when((j >= J_SK) & (j < J_SK + 2))
    def _():
        kv_ref[0] = u


def _in_proj(x, ff, g2, sh, sc, nw, w16, wif16, bif, tl):
    B, L, D = x.shape
    lm = sh.shape[1]
    has_ff = ff is not None
    tlm = 1 if lm == 1 else tl
    row = lambda b, i, j: (b, i, 0)
    mod = (lambda b, i, j: (b, 0, 0)) if lm == 1 else row
    fixed = lambda b, i, j: (0, 0)
    in_specs = [pl.BlockSpec((1, tl, D), row)]
    args = [x]
    if has_ff:
        in_specs += [pl.BlockSpec((1, tl, D), row), pl.BlockSpec((1, tlm, D), mod)]
        args += [ff, g2]
    in_specs += [
        pl.BlockSpec((1, tlm, D), mod),
        pl.BlockSpec((1, tlm, D), mod),
        pl.BlockSpec((1, D), fixed),
        pl.BlockSpec((D, TN_IN), lambda b, i, j: (0, j)),
        pl.BlockSpec((D, LANES), fixed),
        pl.BlockSpec((1, LANES), fixed),
    ]
    args += [sh, sc, nw, w16, wif16, bif]
    out_specs = [
        pl.BlockSpec((1, tl, TN_IN), lambda b, i, j: (b, i, j)),
        pl.BlockSpec((1, tl, W_SB), lambda b, i, j: (b, i, jnp.clip(j - J_SK, 0, 1))),
        pl.BlockSpec((1, tl, LANES), row),
    ]
    out_shape = [
        jax.ShapeDtypeStruct((B, L, U_W), BF16),
        jax.ShapeDtypeStruct((B, L, 2 * W_SB), F32),
        jax.ShapeDtypeStruct((B, L, LANES), F32),
    ]
    if has_ff:
        out_specs.append(pl.BlockSpec((1, tl, D), row))
        out_shape.append(jax.ShapeDtypeStruct((B, L, D), F32))
    return pl.pallas_call(
        functools.partial(_in_proj_kernel, has_ff),
        grid=(B, L // tl, U_W // TN_IN),
        in_specs=in_specs,
        out_specs=out_specs,
        out_shape=out_shape,
        scratch_shapes=[pltpu.VMEM((tl, D), BF16)],
        compiler_params=_cparams(("arbitrary", "arbitrary", "arbitrary")),
    )(*args)


def _mlstm_kernel(valid, q_ref, k_ref, v_ref, o_ref, if_ref, c0_ref, n0_ref, m0_ref, nw_ref,
                  h_ref, co_ref, no_ref, mo_ref, c_scr, n_scr, m_scr, gt_scr):
    hd = pl.program_id(1)
    c = pl.program_id(2)
    cs = M_CHUNK

    @pl.when(c == 0)
    def _():
        c_scr[...] = c0_ref[0, 0]
        n_scr[...] = n0_ref[0, 0]
        m_scr[...] = m0_ref[0, 0]

    q = q_ref[0]
    ksc = k_ref[0].astype(F32) * (DH_M ** -0.5)
    kb = ksc.astype(BF16)
    v = v_ref[0]
    gates = if_ref[0]
    gt_scr[...] = gates.T
    lane = lax.broadcasted_iota(jnp.int32, (cs, LANES), 1)
    i_col = jnp.sum(jnp.where(lane == hd, gates, 0.0), axis=1, keepdims=True)
    f_col = jnp.sum(jnp.where(lane == hd + NH_M, gates, 0.0), axis=1, keepdims=True)
    i_row = gt_scr[pl.ds(hd, 1), :]
    f_row = gt_scr[pl.ds(hd + NH_M, 1), :]
    lf_col = _log_sigmoid(f_col)
    lf_row = _log_sigmoid(f_row)
    t_idx = lax.broadcasted_iota(jnp.int32, (cs, cs), 0)
    s_idx = lax.broadcasted_iota(jnp.int32, (cs, cs), 1)
    if valid < cs:
        col_ok = t_idx[:, :1] < valid
        row_ok = s_idx[:1, :] < valid
        i_col = jnp.where(col_ok, i_col, NEG_BIG)
        lf_col = jnp.where(col_ok, lf_col, 0.0)
        i_row = jnp.where(row_ok, i_row, NEG_BIG)
        lf_row = jnp.where(row_ok, lf_row, 0.0)
    causal = s_idx <= t_idx
    F_col = jnp.sum(jnp.where(causal, lf_row, 0.0), axis=1, keepdims=True)
    F_row = jnp.sum(jnp.where(t_idx <= s_idx, lf_col, 0.0), axis=0, keepdims=True)
    F_end = jnp.sum(lf_row, axis=1, keepdims=True)
    m_prev = m_scr[:, 0:1]

    dlog = jnp.where(causal, F_col - F_row + i_row, -jnp.inf)
    b_inter = F_col + m_prev
    m_t = jnp.maximum(b_inter, jnp.max(dlog, axis=1, keepdims=True))
    s = _dot_nt(q, kb) * jnp.exp(dlog - m_t)
    a_inter = jnp.exp(b_inter - m_t)
    Cmat = c_scr[...]
    nvec = n_scr[...]
    num = _dot(s.astype(BF16), v) + a_inter * _dot_nt(q, Cmat.astype(BF16))
    qn = jnp.sum(q.astype(F32) * nvec, axis=1, keepdims=True)
    den = jnp.sum(s, axis=1, keepdims=True) + a_inter * qn
    den = jnp.maximum(jnp.abs(den), jnp.exp(-m_t))
    h = num / den

    dec_row = F_end - F_row + i_row
    dec_col = F_end - F_col + i_col
    m_new = jnp.maximum(F_end + m_prev, jnp.max(dec_row, axis=1, keepdims=True))
    w_col = jnp.exp(dec_col - m_new)
    keep = jnp.exp(F_end + m_prev - m_new)
    c_scr[...] = keep * Cmat + _dot_tn((w_col * v.astype(F32)).astype(BF16), kb)
    n_scr[...] = keep * nvec + jnp.sum(w_col * ksc, axis=0, keepdims=True)
    m_scr[...] = jnp.broadcast_to(m_new, m_scr.shape)

    mu = jnp.mean(h, axis=-1, keepdims=True)
    var = jnp.mean(jnp.square(h - mu), axis=-1, keepdims=True)
    hn = (h - mu) * lax.rsqrt(var + EPS) * nw_ref[0]
    h_ref[0] = (hn * _sigmoid(o_ref[0].astype(F32))).astype(BF16)

    @pl.when(c == pl.num_programs(2) - 1)
    def _():
        co_ref[0, 0] = c_scr[...]
        no_ref[0, 0] = n_scr[...]
        mo_ref[0, 0] = m_scr[...]


def _mlstm(um, if32, C0, n0, m0, mnorm_w, valid):
    B, L = um.shape[0], um.shape[1]
    nc = L // M_CHUNK
    blk = lambda off: pl.BlockSpec((1, M_CHUNK, DH_M), lambda b, h, c: (b, c, off + h))
    st = lambda r: pl.BlockSpec((1, 1, r, DH_M), lambda b, h, c: (b, h, 0, 0))
    return pl.pallas_call(
        functools.partial(_mlstm_kernel, valid),
        grid=(B, NH_M, nc),
        in_specs=[
            blk(0), blk(NH_M), blk(2 * NH_M), blk(3 * NH_M),
            pl.BlockSpec((1, M_CHUNK, LANES), lambda b, h, c: (b, c, 0)),
            st(DH_M), st(1), st(1),
            pl.BlockSpec((1, 1, DH_M), lambda b, h, c: (h, 0, 0)),
        ],
        out_specs=[
            pl.BlockSpec((1, M_CHUNK, DH_M), lambda b, h, c: (b, c, h)),
            st(DH_M), st(1), st(1),
        ],
        out_shape=[
            jax.ShapeDtypeStruct((B, L, W_M), BF16),
            jax.ShapeDtypeStruct((B, NH_M, DH_M, DH_M), F32),
            jax.ShapeDtypeStruct((B, NH_M, 1, DH_M), F32),
            jax.ShapeDtypeStruct((B, NH_M, 1, DH_M), F32),
        ],
        scratch_shapes=[
            pltpu.VMEM((DH_M, DH_M), F32),
            pltpu.VMEM((1, DH_M), F32),
            pltpu.VMEM((1, DH_M), F32),
            pltpu.VMEM((LANES, M_CHUNK), F32),
        ],
        compiler_params=_cparams(("arbitrary", "arbitrary", "arbitrary")),
    )(um, um, um, um, if32, C0, n0.reshape(B, NH_M, 1, DH_M),
      jnp.broadcast_to(m0[:, :, None, None], (B, NH_M, 1, DH_M)),
      mnorm_w.reshape(NH_M, 1, DH_M))


def _conv_kernel(tl, rb, a_ref, g_ref, st_ref, w_ref, b_ref, lw_ref, lb_ref,
                 o_ref, so_ref, buf, shifted):
    i = pl.program_id(1)
    sub = 8

    @pl.when(i == 0)
    def _():
        buf[0:CONV_HIST, :] = st_ref[0]

    a = a_ref[0].astype(F32) * _sigmoid(g_ref[0].astype(F32))
    buf[CONV_HIST:CONV_HIST + tl, :] = a
    pad = CONV_HIST - (CONV_W - 1)
    span = shifted.shape[1]
    for s in range(1, sub):
        shifted[s - 1] = buf[s:s + span, :]

    def block(r, carry):
        base = pl.multiple_of(r * rb, rb)
        acc = jnp.broadcast_to(b_ref[...], (rb, CONV_CH))
        for j in range(CONV_W):
            qq, s = divmod(j + pad, sub)
            start = pl.multiple_of(base + sub * qq, sub)
            if s == 0:
                rows = buf[pl.ds(start, rb), :]
            else:
                rows = shifted[s - 1, pl.ds(start, rb), :]
            acc = acc + w_ref[j:j + 1, :] * rows
        mu = jnp.mean(acc, axis=-1, keepdims=True)
        var = jnp.mean(jnp.square(acc - mu), axis=-1, keepdims=True)
        y = (acc - mu) * lax.rsqrt(var + EPS) * lw_ref[...] + lb_ref[...]
        o_ref[0, pl.ds(base, rb), :] = (y * _sigmoid(y)).astype(o_ref.dtype)
        return carry

    lax.fori_loop(0, tl // rb, block, 0)
    tail = buf[tl:tl + CONV_HIST, :]
    buf[0:CONV_HIST, :] = tail

    @pl.when(i == pl.num_programs(1) - 1)
    def _():
        so_ref[0] = tail


def _conv(ua, ug, off_a, off_g, state, dw_w, dw_b, ln_w, ln_b, tl):
    B, L = ua.shape[0], ua.shape[1]
    rb = min(tl, 32)
    pad = CONV_HIST - (CONV_W - 1)
    st = jnp.pad(state, ((0, 0), (pad, 0), (0, 0)))
    w = jnp.pad(dw_w, ((0, CONV_HIST - CONV_W), (0, 0)))
    vec = lambda b, i: (0, 0)
    out, so = pl.pallas_call(
        functools.partial(_conv_kernel, tl, rb),
        grid=(B, L // tl),
        in_specs=[
            pl.BlockSpec((1, tl, CONV_CH), lambda b, i: (b, i, off_a)),
            pl.BlockSpec((1, tl, CONV_CH), lambda b, i: (b, i, off_g)),
            pl.BlockSpec((1, CONV_HIST, CONV_CH), lambda b, i: (b, 0, 0)),
            pl.BlockSpec((CONV_HIST, CONV_CH), vec),
            pl.BlockSpec((1, CONV_CH), vec),
            pl.BlockSpec((1, CONV_CH), vec),
            pl.BlockSpec((1, CONV_CH), vec),
        ],
        out_specs=[
            pl.BlockSpec((1, tl, CONV_CH), lambda b, i: (b, i, 0)),
            pl.BlockSpec((1, CONV_HIST, CONV_CH), lambda b, i: (b, 0, 0)),
        ],
        out_shape=[
            jax.ShapeDtypeStruct((B, L, CONV_CH), ua.dtype),
            jax.ShapeDtypeStruct((B, CONV_HIST, CONV_CH), F32),
        ],
        scratch_shapes=[pltpu.VMEM((CONV_HIST + tl, CONV_CH), F32),
                        pltpu.VMEM((7, CONV_HIST + tl - 8, CONV_CH), F32)],
        compiler_params=_cparams(("arbitrary", "arbitrary")),
    )(ua, ug, st, w, dw_b.reshape(1, -1), ln_w.reshape(1, -1), ln_b.reshape(1, -1))
    return out, so[:, pad:, :]


def _sb_tri(tk):
    r = lax.broadcasted_iota(jnp.int32, (tk, tk + LANES), 0)
    c = lax.broadcasted_iota(jnp.int32, (tk, tk + LANES), 1)
    return jnp.where((c >= tk) | (r > c), 1.0, 0.0).astype(BF16)


SB_HEADS = 4
SB_W = SB_HEADS * DH_SB
SB_TQ = 256
SB_CB = 256


def _sb_prompt_kernel(tq, tk, bias_ref, tri_ref, q_ref, k_ref, v_ref, o_ref, acc_scr, car_scr):
    p = pl.program_id(1)
    qi = pl.program_id(2)
    cb = min(SB_CB, tk)
    nb = tk // cb
    nsub = tq // LANES
    ng = nsub * SB_HEADS
    lane = lax.broadcasted_iota(jnp.int32, (1, SB_W), 1)
    q2 = q_ref[0].astype(F32) * (DH_SB ** -0.5)
    qs = jnp.concatenate(
        [jnp.where((lane // DH_SB) == hh, q2[r * LANES:(r + 1) * LANES], 0.0).astype(BF16)
         for r in range(nsub) for hh in range(SB_HEADS)], axis=0)
    bias = [bias_ref[SB_HEADS * p + hh] for hh in range(SB_HEADS)]
    acc_scr[...] = jnp.zeros_like(acc_scr)
    car_scr[...] = jnp.zeros_like(car_scr)
    rr = lax.broadcasted_iota(jnp.int32, (LANES, tk), 0)
    cc = lax.broadcasted_iota(jnp.int32, (LANES, tk), 1)
    rows = lambda a, i: a[i * LANES:(i + 1) * LANES]

    def tile(kt, masked):
        start = pl.multiple_of(kt * tk, tk)
        ktile = k_ref[0, pl.ds(start, tk), :]
        vtile = v_ref[0, pl.ds(start, tk), :]
        if masked:
            visible = [(cc + kt * tk) < (rr + (qi * tq + r * LANES)) for r in range(nsub)]
        z_all = _dot_nt(qs, ktile)
        log_beta, log_1m = [], []
        for g in range(ng):
            z = rows(z_all, g) + bias[g % SB_HEADS]
            sp = jnp.log(1.0 + jnp.exp(-jnp.abs(z)))
            lb = jnp.minimum(z, 0.0) - sp
            l1 = lb - z
            if masked:
                l1 = jnp.where(visible[g // SB_HEADS], l1, 0.0)
            log_beta.append(lb)
            log_1m.append(l1)
        stack = jnp.concatenate([log_1m[g][:, j * cb:(j + 1) * cb].astype(BF16)
                                 for j in range(nb) for g in range(ng)], axis=0)
        later_all = _dot(stack, tri_ref[...])
        a_rows = []
        for g in range(ng):
            car = car_scr[g]
            parts = [None] * nb
            for j in reversed(range(nb)):
                sl = slice(j * cb, (j + 1) * cb)
                later = rows(later_all, j * ng + g)
                parts[j] = jnp.exp(log_beta[g][:, sl] + later + jnp.tile(car, (1, cb // LANES)))
                car = car + jnp.sum(log_1m[g][:, sl], axis=1, keepdims=True)
            car_scr[g] = car
            A = jnp.concatenate(parts, axis=1)
            if masked:
                A = jnp.where(visible[g // SB_HEADS], A, 0.0)
            a_rows.append(A.astype(BF16))
        acc_scr[...] = acc_scr[...] + _dot(jnp.concatenate(a_rows, axis=0), vtile)

    kt_d = (qi * tq) // tk
    tile(kt_d, True)

    def body(it, carry):
        tile(kt_d - 1 - it, False)
        return carry

    lax.fori_loop(0, kt_d, body, 0)
    head = lax.broadcasted_iota(jnp.int32, (LANES, SB_W), 1) // DH_SB
    for r in range(nsub):
        out = acc_scr[(r * SB_HEADS) * LANES:(r * SB_HEADS + 1) * LANES]
        for hh in range(1, SB_HEADS):
            g = r * SB_HEADS + hh
            out = jnp.where(head == hh, acc_scr[g * LANES:(g + 1) * LANES], out)
        o_ref[0, r * LANES:(r + 1) * LANES, :] = out.astype(BF16)


def _sb_prompt(u16, sb_b):
    B, L = u16.shape[0], u16.shape[1]
    tq = min(SB_TQ, L)
    tk = min(512, L)
    cb = min(SB_CB, tk)
    c0 = U_S // SB_W
    nq = W_SB // SB_W
    r = lax.broadcasted_iota(jnp.int32, (cb, cb), 0)
    c = lax.broadcasted_iota(jnp.int32, (cb, cb), 1)
    tri = jnp.where(r > c, 1.0, 0.0).astype(BF16)
    ng = (tq // LANES) * SB_HEADS
    return pl.pallas_call(
        functools.partial(_sb_prompt_kernel, tq, tk),
        grid=(B, nq, L // tq),
        in_specs=[
            pl.BlockSpec(memory_space=pltpu.SMEM),
            pl.BlockSpec((cb, cb), lambda b, p, i: (0, 0)),
            pl.BlockSpec((1, tq, SB_W), lambda b, p, i: (b, i, c0 + p)),
            pl.BlockSpec((1, L, SB_W), lambda b, p, i: (b, 0, c0 + nq + p)),
            pl.BlockSpec((1, L, SB_W), lambda b, p, i: (b, 0, c0 + 2 * nq + p)),
        ],
        out_specs=pl.BlockSpec((1, tq, SB_W), lambda b, p, i: (b, i, p)),
        out_shape=jax.ShapeDtypeStruct((B, L, W_SB), BF16),
        scratch_shapes=[pltpu.VMEM((ng * LANES, SB_W), F32),
                        pltpu.VMEM((ng, LANES, LANES), F32)],
        compiler_params=_cparams(("arbitrary", "arbitrary", "arbitrary")),
    )(sb_b, tri, u16, u16, u16)


PAGES_PER_STEP = 8


def _sb_sample_kernel(lq, pt_ref, qbd_ref, bias_ref, tri_ref, kn_ref, vn_ref, *refs):
    g = PAGES_PER_STEP
    k_refs = refs[:g]
    v_refs = refs[g:2 * g]
    o_ref = refs[2 * g]
    acc_scr, car_scr = refs[2 * g + 1:]
    s = pl.program_id(1)
    nr = NH_SB * lq
    qbd = qbd_ref[0]
    bias = bias_ref[...]
    tri = tri_ref[...]

    def scores(k16, visible):
        z = _dot_nt(qbd, k16) + jnp.tile(bias, (1, k16.shape[0] // LANES))
        sp = jnp.log(1.0 + jnp.exp(-jnp.abs(z)))
        log_beta = jnp.minimum(z, 0.0) - sp
        log_1m = log_beta - z
        if visible is not None:
            log_1m = jnp.where(visible, log_1m, 0.0)
        return log_beta, log_1m

    def weights(log_beta, log_1m, car):
        hi, lo = _split_bf16(log_1m)
        parts = []
        for j in range(log_1m.shape[1] // LANES):
            sl = slice(j * LANES, (j + 1) * LANES)
            cum = _dot(hi[:, sl], tri) + _dot(lo[:, sl], tri)
            parts.append(jnp.exp(log_beta[:, sl] + cum[:, :LANES] + car))
            car = car + cum[:, LANES:]
        return parts, car

    @pl.when(s == 0)
    def _():
        r = lax.broadcasted_iota(jnp.int32, (nr, PAGE_SIZE), 0)
        c = lax.broadcasted_iota(jnp.int32, (nr, PAGE_SIZE), 1)
        visible = c < (r % lq)
        log_beta, log_1m = scores(kn_ref[0].astype(BF16), visible)
        parts, car = weights(log_beta, log_1m, jnp.zeros((nr, LANES), F32))
        A = jnp.where(visible, parts[0], 0.0)
        acc_scr[...] = _dot(A.astype(BF16), vn_ref[0].astype(BF16))
        car_scr[...] = car

    k16 = jnp.concatenate([k_refs[gi][0].astype(BF16) for gi in range(g)], axis=0)
    v16 = jnp.concatenate([v_refs[gi][0].astype(BF16) for gi in range(g)], axis=0)
    log_beta, log_1m = scores(k16, None)
    parts, car = weights(log_beta, log_1m, car_scr[...])
    A = jnp.concatenate(parts, axis=1).astype(BF16)
    acc = acc_scr[...] + _dot(A, v16)
    acc_scr[...] = acc
    car_scr[...] = car

    @pl.when(s == pl.num_programs(1) - 1)
    def _():
        r = lax.broadcasted_iota(jnp.int32, (nr, W_SB), 0)
        c = lax.broadcasted_iota(jnp.int32, (nr, W_SB), 1)
        m = jnp.where((r // lq) == (c // DH_SB), acc, 0.0)
        out = m[0:lq]
        for hh in range(1, NH_SB):
            out = out + m[hh * lq:(hh + 1) * lq]
        o_ref[0] = out


def _sb_sample(q16, k_new, v_new, cache_k, cache_v, page_table, sb_b):
    B, lq, _ = q16.shape
    n_pages = page_table.shape[1]
    g = PAGES_PER_STEP
    nr = NH_SB * lq
    k_new = jnp.pad(k_new, ((0, 0), (0, PAGE_SIZE - lq), (0, 0)))
    v_new = jnp.pad(v_new, ((0, 0), (0, PAGE_SIZE - lq), (0, 0)))
    qh = (q16.astype(F32) * (DH_SB ** -0.5)).reshape(B, lq, NH_SB, DH_SB)
    qbd = jnp.einsum('bihd,hg->bhigd', qh, jnp.eye(NH_SB, dtype=F32)).reshape(B, nr, W_SB).astype(BF16)
    bias = jnp.broadcast_to(jnp.repeat(sb_b.astype(F32), lq)[:, None], (nr, LANES))

    def page_spec(gi):
        return pl.BlockSpec((1, PAGE_SIZE, W_SB),
                            lambda b, s, pt: (pt[b, n_pages - 1 - (s * g + gi)], 0, 0))

    fixed = lambda b, s, pt: (0, 0)
    seq = lambda b, s, pt: (b, 0, 0)
    grid_spec = pltpu.PrefetchScalarGridSpec(
        num_scalar_prefetch=1,
        grid=(B, n_pages // g),
        in_specs=[
            pl.BlockSpec((1, nr, W_SB), seq),
            pl.BlockSpec((nr, LANES), fixed),
            pl.BlockSpec((PAGE_SIZE, PAGE_SIZE + LANES), fixed),
            pl.BlockSpec((1, PAGE_SIZE, W_SB), seq),
            pl.BlockSpec((1, PAGE_SIZE, W_SB), seq),
        ] + [page_spec(gi) for gi in range(g)] * 2,
        out_specs=pl.BlockSpec((1, lq, W_SB), seq),
        scratch_shapes=[pltpu.VMEM((nr, W_SB), F32), pltpu.VMEM((nr, LANES), F32)],
    )
    return pl.pallas_call(
        functools.partial(_sb_sample_kernel, lq),
        grid_spec=grid_spec,
        out_shape=jax.ShapeDtypeStruct((B, lq, W_SB), F32),
        compiler_params=_cparams(("arbitrary", "arbitrary")),
    )(page_table, qbd, bias, _sb_tri(PAGE_SIZE), k_new, v_new, *([cache_k] * g), *([cache_v] * g))


def _merge_kernel(hm_ref, hc_ref, ha_ref, gm_ref, gc_ref, ga_ref, x_ref, g1_ref, sh_ref, sc_ref,
                  wpm_ref, wpc_ref, wpa_ref, wo_ref, nw_ref, rwh_ref, rwl_ref, rb_ref,
                  xo_ref, h2_ref, gate_ref):
    y = _sigmoid(gm_ref[0].astype(F32)) * _dot(hm_ref[0], wpm_ref[...])
    y = y + _sigmoid(gc_ref[0].astype(F32)) * _dot(hc_ref[0], wpc_ref[...])
    y = y + _sigmoid(ga_ref[0].astype(F32)) * _dot(ha_ref[0], wpa_ref[...])
    mix = _dot(y.astype(BF16), wo_ref[...])
    x = x_ref[0] + g1_ref[0] * mix
    xo_ref[0] = x
    ms = jnp.mean(x * x, axis=-1, keepdims=True)
    h = x * lax.rsqrt(ms + EPS) * nw_ref[...]
    h = h * (1.0 + sc_ref[0]) + sh_ref[0]
    h_hi, h_lo = _split_bf16(h)
    h2_ref[0] = h_hi
    logits = (_dot(h_hi, rwh_ref[...]) + _dot(h_lo, rwh_ref[...]) + _dot(h_hi, rwl_ref[...])
              + rb_ref[...])
    lane = lax.broadcasted_iota(jnp.int32, logits.shape, 1).astype(F32)
    work = logits
    sel = jnp.zeros(logits.shape, F32)
    vmax = None
    for _ in range(TOP_K):
        mx = jnp.max(work, axis=-1, keepdims=True)
        if vmax is None:
            vmax = mx
        first = jnp.min(jnp.where(work == mx, lane, float(LANES)), axis=-1, keepdims=True)
        pick = lane == first
        sel = jnp.where(pick, 1.0, sel)
        work = jnp.where(pick, -jnp.inf, work)
    e = sel * jnp.exp(logits - vmax)
    gate_ref[0] = e / jnp.sum(e, axis=-1, keepdims=True)


def _merge(hm, hc, ha, u16, x, g1, sh2, sc2, wpm, wpc, wpa, wo, nw, rw_hi, rw_lo, rb, tl):
    B, L, D = x.shape
    lm = g1.shape[1]
    tlm = 1 if lm == 1 else tl
    row = lambda b, i: (b, i, 0)
    mod = (lambda b, i: (b, 0, 0)) if lm == 1 else row
    fixed = lambda b, i: (0, 0)
    gb = U_G // D
    half = lambda: pl.BlockSpec((1, tl, W_M), row)
    gate = lambda n: pl.BlockSpec((1, tl, D), lambda b, i: (b, i, gb + n))
    modspec = lambda: pl.BlockSpec((1, tlm, D), mod)
    return pl.pallas_call(
        _merge_kernel,
        grid=(B, L // tl),
        in_specs=[
            half(), half(), half(), gate(0), gate(1), gate(2),
            pl.BlockSpec((1, tl, D), row), modspec(), modspec(), modspec(),
            pl.BlockSpec((W_M, D), fixed), pl.BlockSpec((CONV_CH, D), fixed),
            pl.BlockSpec((W_SB, D), fixed), pl.BlockSpec((D, D), fixed),
            pl.BlockSpec((1, D), fixed),
            pl.BlockSpec((D, LANES), fixed), pl.BlockSpec((D, LANES), fixed),
            pl.BlockSpec((1, LANES), fixed),
        ],
        out_specs=[
            pl.BlockSpec((1, tl, D), row),
            pl.BlockSpec((1, tl, D), row),
            pl.BlockSpec((1, tl, LANES), row),
        ],
        out_shape=[
            jax.ShapeDtypeStruct((B, L, D), F32),
            jax.ShapeDtypeStruct((B, L, D), BF16),
            jax.ShapeDtypeStruct((B, L, LANES), F32),
        ],
        compiler_params=_cparams(("arbitrary", "arbitrary")),
    )(hm, hc, ha, u16, u16, u16, x, g1, sh2, sc2, wpm, wpc, wpa, wo, nw, rw_hi, rw_lo, rb)


MOE_ROWS = 192


def _moe_kernel(cnt_ref, h_ref, gt_ref, u_ref, w1_ref, b1_ref, w2_ref, b2_ref, o_ref, rank_scr):
    i = pl.program_id(0)
    e = pl.program_id(1)
    tm = h_ref.shape[0]

    @pl.when(e == 0)
    def _():
        o_ref[...] = jnp.zeros_like(o_ref)
        routed = gt_ref[...] > 0.0
        before = _dot(jnp.where(routed, 1.0, 0.0).astype(BF16), u_ref[...])
        rank_scr[...] = jnp.where(routed, before, -1.0)

    rank = rank_scr[pl.ds(e, 1), :]
    gate = gt_ref[pl.ds(e, 1), :]
    slot = lax.broadcasted_iota(jnp.int32, (MOE_ROWS, tm), 0).astype(F32)

    def chunk(c, carry):
        sel = jnp.where((rank - (c * MOE_ROWS).astype(F32)) == slot, 1.0, 0.0)
        sel16 = sel.astype(BF16)
        g_row = jnp.sum(sel * gate, axis=1, keepdims=True)
        x = _dot(sel16, h_ref[...]).astype(BF16)
        z = _dot(x, w1_ref[0]) + b1_ref[0]
        z_glu = jnp.minimum(z[:, :D_FF], SWIGLU_LIMIT)
        z_lin = jnp.clip(z[:, D_FF:], -SWIGLU_LIMIT, SWIGLU_LIMIT)
        act = z_glu * _sigmoid(SWIGLU_ALPHA * z_glu) * (z_lin + 1.0)
        y = _dot(act.astype(BF16), w2_ref[0]) + b2_ref[0]
        o_ref[...] += _dot_tn(sel16, (g_row * y).astype(BF16))
        return carry

    n_routed = cnt_ref[i, e]
    lax.fori_loop(0, (n_routed + MOE_ROWS - 1) // MOE_ROWS, chunk, 0)


def _moe(h2, gates, w1, b1, w2, b2, tm):
    N, D = h2.shape
    gt = gates.T
    cnt = jnp.sum((gates > 0.0).reshape(N // tm, tm, LANES), axis=1).astype(jnp.int32)
    r = lax.broadcasted_iota(jnp.int32, (tm, tm), 0)
    c = lax.broadcasted_iota(jnp.int32, (tm, tm), 1)
    before = jnp.where(r < c, 1.0, 0.0).astype(BF16)
    grid_spec = pltpu.PrefetchScalarGridSpec(
        num_scalar_prefetch=1,
        grid=(N // tm, N_EXP),
        in_specs=[
            pl.BlockSpec((tm, D), lambda i, e, cnt: (i, 0)),
            pl.BlockSpec((LANES, tm), lambda i, e, cnt: (0, i)),
            pl.BlockSpec((tm, tm), lambda i, e, cnt: (0, 0)),
            pl.BlockSpec((1, D, 2 * D_FF), lambda i, e, cnt: (e, 0, 0)),
            pl.BlockSpec((1, 1, 2 * D_FF), lambda i, e, cnt: (e, 0, 0)),
            pl.BlockSpec((1, D_FF, D), lambda i, e, cnt: (e, 0, 0)),
            pl.BlockSpec((1, 1, D), lambda i, e, cnt: (e, 0, 0)),
        ],
        out_specs=pl.BlockSpec((tm, D), lambda i, e, cnt: (i, 0)),
        scratch_shapes=[pltpu.VMEM((LANES, tm), F32)],
    )
    return pl.pallas_call(
        _moe_kernel,
        grid_spec=grid_spec,
        out_shape=jax.ShapeDtypeStruct((N, D), F32),
        compiler_params=_cparams(("arbitrary", "arbitrary")),
    )(cnt, h2, gt, before, w1, b1.reshape(N_EXP, 1, 2 * D_FF), w2, b2.reshape(N_EXP, 1, D))


def _final_kernel(x_ref, ff_ref, g2_ref, w_ref, o_ref):
    x = x_ref[0] + g2_ref[0] * ff_ref[0]
    ms = jnp.mean(x * x, axis=-1, keepdims=True)
    o_ref[0] = x * lax.rsqrt(ms + EPS) * w_ref[...]


def _final(x, ff, g2, w, tl):
    B, L, D = x.shape
    lm = g2.shape[1]
    tlm = 1 if lm == 1 else tl
    row = lambda b, i: (b, i, 0)
    mod = (lambda b, i: (b, 0, 0)) if lm == 1 else row
    return pl.pallas_call(
        _final_kernel,
        grid=(B, L // tl),
        in_specs=[
            pl.BlockSpec((1, tl, D), row), pl.BlockSpec((1, tl, D), row),
            pl.BlockSpec((1, tlm, D), mod), pl.BlockSpec((1, D), lambda b, i: (0, 0)),
        ],
        out_specs=pl.BlockSpec((1, tl, D), row),
        out_shape=jax.ShapeDtypeStruct((B, L, D), F32),
        compiler_params=_cparams(("arbitrary", "arbitrary")),
    )(x, ff, g2, w.reshape(1, D))


def _prep_layer_weights(l, w_in, mlstm_gate_b, router_w, router_b):
    wl = w_in[l]
    w_main = jnp.concatenate(
        [wl[:, OFF_MQ:OFF_MIF], wl[:, OFF_CG:OFF_SQ], wl[:, OFF_G:], wl[:, OFF_SQ:OFF_G]], axis=1)
    wif = jnp.pad(wl[:, OFF_MIF:OFF_CG], ((0, 0), (0, LANES - 2 * NH_M)))
    bif = jnp.pad(mlstm_gate_b[l], (0, LANES - 2 * NH_M)).reshape(1, LANES)
    rw = jnp.pad(router_w[l], ((0, 0), (0, LANES - N_EXP)))
    rw_hi = rw.astype(BF16)
    rw_lo = (rw - rw_hi.astype(F32)).astype(BF16)
    rb = jnp.pad(router_b[l], (0, LANES - N_EXP), constant_values=NEG_BIG).reshape(1, LANES)
    return w_main.astype(BF16), wif.astype(BF16), bif, rw_hi, rw_lo, rb


def _trunk(x, mods, C0, n0, m0, conv0, paged, W, tl, tl_mg, tm):
    B, L, D = x.shape
    sh1, sc1, g1, sh2, sc2, g2 = mods
    ks, vs, Cs, ns, ms, bufs = [], [], [], [], [], []
    ff = None
    for l in range(DEPTH):
        w16, wif16, bif, rw_hi, rw_lo, rb = W['prep'][l]
        nw1 = W['norm1_w'][l].reshape(1, D)
        if ff is None:
            u16, kv32, if32 = _in_proj(x, None, None, sh1[l], sc1[l], nw1, w16, wif16, bif, tl)
        else:
            u16, kv32, if32, x = _in_proj(x, ff, g2[l - 1], sh1[l], sc1[l], nw1, w16, wif16, bif, tl)
        if paged is None:
            hm, Cn, nn, mn = _mlstm(u16, if32, C0[l], n0[l], m0[l], W['mlstm_norm_w'][l], M_CHUNK)
            hc, buf = _conv(u16, u16, U_CG // CONV_CH, U_CG // CONV_CH + 1, conv0[l],
                            W['conv_dw_w'][l], W['conv_dw_b'][l], W['conv_ln_w'][l], W['conv_ln_b'][l],
                            min(512, L))
            ha = _sb_prompt(u16, W['sb_bias'][l])
            k_new = kv32[..., :W_SB].reshape(B, L, NH_SB, DH_SB)
            v_new = kv32[..., W_SB:].reshape(B, L, NH_SB, DH_SB)
        else:
            cache_k, cache_v, page_table, bs, ls = paged
            us = u16.reshape(bs, ls, U_W)
            padl = M_CHUNK - ls
            um = jnp.pad(us[..., :4 * W_M], ((0, 0), (0, padl), (0, 0)))
            ifp = jnp.pad(if32.reshape(bs, ls, LANES), ((0, 0), (0, padl), (0, 0)))
            hm, Cn, nn, mn = _mlstm(um, ifp, C0[l], n0[l], m0[l], W['mlstm_norm_w'][l], ls)
            hm = hm[:, :ls].reshape(B, L, W_M)
            ucg = us[..., U_CG:U_CG + 2 * CONV_CH].astype(F32)
            hc, buf = _conv(ucg, ucg, 0, 1, conv0[l],
                            W['conv_dw_w'][l], W['conv_dw_b'][l], W['conv_ln_w'][l], W['conv_ln_b'][l], ls)
            hc = hc.reshape(B, L, CONV_CH).astype(BF16)
            kvs = kv32.reshape(bs, ls, 2 * W_SB)
            ha = _sb_sample(us[..., U_S:U_S + W_SB], kvs[..., :W_SB], kvs[..., W_SB:],
                            cache_k[l].reshape(-1, PAGE_SIZE, W_SB), cache_v[l].reshape(-1, PAGE_SIZE, W_SB),
                            page_table, W['sb_bias'][l]).reshape(B, L, W_SB).astype(BF16)
            k_new = kvs[..., :W_SB].reshape(bs, ls, NH_SB, DH_SB)
            v_new = kvs[..., W_SB:].reshape(bs, ls, NH_SB, DH_SB)
        x, h2, gates = _merge(hm, hc, ha, u16, x, g1[l], sh2[l], sc2[l],
                              W['w_pm16'][l], W['w_pc16'][l], W['w_pa16'][l], W['w_out16'][l],
                              W['norm2_w'][l].reshape(1, D), rw_hi, rw_lo, rb, tl_mg)
        ff = _moe(h2.reshape(B * L, D), gates.reshape(B * L, LANES),
                  W['exp_w1_16'][l], W['exp_b1'][l], W['exp_w2_16'][l], W['exp_b2'][l], tm).reshape(B, L, D)
        ks.append(k_new)
        vs.append(v_new)
        Cs.append(Cn)
        ns.append(nn[:, :, 0, :])
        ms.append(mn[:, :, 0, 0])
        bufs.append(buf)
    y = _final(x, ff, g2[DEPTH - 1], W['final_w'], tl)
    return y, jnp.stack(ks), jnp.stack(vs), jnp.stack(Cs), jnp.stack(ns), jnp.stack(ms), jnp.stack(bufs)


def kernel(x_prompt, x_sample, c_prompt, c_sample, cache_k, cache_v, state_C, state_n, state_m, state_conv, page_table, w_ada, b_ada, norm1_w, norm2_w, w_in, mlstm_gate_b, mlstm_norm_w, w_pm, conv_dw_w, conv_dw_b, conv_ln_w, conv_ln_b, w_pc, w_pa, sb_bias, w_out, router_w, router_b, exp_w1, exp_b1, exp_w2, exp_b2, final_w):
    BP, LP, D = x_prompt.shape
    BS, LS, _ = x_sample.shape
    W = dict(norm1_w=norm1_w, norm2_w=norm2_w, mlstm_norm_w=mlstm_norm_w,
             conv_dw_w=conv_dw_w, conv_dw_b=conv_dw_b, conv_ln_w=conv_ln_w, conv_ln_b=conv_ln_b,
             sb_bias=sb_bias, exp_b1=exp_b1, exp_b2=exp_b2, final_w=final_w,
             w_pm16=w_pm.astype(BF16), w_pc16=w_pc.astype(BF16), w_pa16=w_pa.astype(BF16),
             w_out16=w_out.astype(BF16),
             exp_w1_16=_cast_bf16(exp_w1, 1024), exp_w2_16=_cast_bf16(exp_w2, 2048))
    W['prep'] = [_prep_layer_weights(l, w_in, mlstm_gate_b, router_w, router_b) for l in range(DEPTH)]

    nb = BP + BS
    nbp = -(-nb // 8) * 8
    c_all = jnp.pad(jnp.concatenate([c_prompt, c_sample], axis=0), ((0, nbp - nb), (0, 0)))
    mod = _adaln(c_all, w_ada, b_ada)
    mods_p = [mod[:, :BP, None, i * D:(i + 1) * D] for i in range(6)]
    mods_s = [jnp.repeat(mod[:, BP:nb, i * D:(i + 1) * D], LS, axis=1)[:, None] for i in range(6)]

    zC = jnp.zeros((DEPTH, BP, NH_M, DH_M, DH_M), F32)
    zn = jnp.zeros((DEPTH, BP, NH_M, DH_M), F32)
    zm = jnp.zeros((DEPTH, BP, NH_M), F32)
    zconv = jnp.zeros((DEPTH, BP, CONV_W - 1, CONV_CH), F32)
    yp, *outs_p = _trunk(x_prompt, mods_p, zC, zn, zm, zconv, None, W,
                         min(1024, LP), min(512, LP), min(1024, BP * LP))
    ys, *outs_s = _trunk(x_sample.reshape(1, BS * LS, D), mods_s, state_C, state_n, state_m, state_conv,
                         (cache_k, cache_v, page_table, BS, LS), W, BS * LS, BS * LS, BS * LS)
    return (yp, ys.reshape(BS, LS, D)) + tuple(outs_p) + tuple(outs_s)
```

```python
import functools

import jax
import jax.numpy as jnp
from jax import lax
from jax.experimental import pallas as pl
from jax.experimental.pallas import tpu as pltpu

F32 = jnp.float32
BF16 = jnp.bfloat16

D_MODEL = 1024
DEPTH = 4
NH_M = 4
DH_M = 128
W_M = NH_M * DH_M
M_CHUNK = 128
CONV_CH = 512
CONV_W = 31
CONV_HIST = 32
NH_SB = 8
DH_SB = 64
W_SB = NH_SB * DH_SB
PAGE_SIZE = 128
N_EXP = 32
TOP_K = 4
D_FF = D_MODEL
SWIGLU_ALPHA = 1.702
SWIGLU_LIMIT = 7.0
EPS = 1e-5
LANES = 128
NEG_BIG = -1e30

OFF_MQ = 0
OFF_MIF = 4 * W_M
OFF_CG = OFF_MIF + 2 * NH_M
OFF_SQ = OFF_CG + 2 * CONV_CH
OFF_G = OFF_SQ + 3 * W_SB
D_IN = OFF_G + 3 * D_MODEL

U_M = 0
U_CG = 2048
U_G = 3072
U_S = 6144
U_W = 7680
TN_IN = 512
J_SK = (U_S + W_SB) // TN_IN

VMEM_LIMIT = 56 * 1024 * 1024


def _cparams(sem):
    return pltpu.CompilerParams(dimension_semantics=sem, vmem_limit_bytes=VMEM_LIMIT)


def _log_sigmoid(x):
    return jnp.minimum(x, 0.0) - jnp.log(1.0 + jnp.exp(-jnp.abs(x)))


def _sigmoid(x):
    return 1.0 / (1.0 + jnp.exp(-x))


def _dot(a, b):
    return jnp.dot(a, b, preferred_element_type=F32)


def _dot_nt(a, b):
    return lax.dot_general(a, b, (((1,), (1,)), ((), ())), preferred_element_type=F32)


def _dot_tn(a, b):
    return lax.dot_general(a, b, (((0,), (0,)), ((), ())), preferred_element_type=F32)


def _split_bf16(x):
    hi = x.astype(BF16)
    lo = (x - hi.astype(F32)).astype(BF16)
    return hi, lo


def _cast_kernel(x_ref, o_ref):
    o_ref[...] = x_ref[...].astype(BF16)


def _cast_bf16(w, rows):
    flat = w.reshape(-1, w.shape[-1])
    n, c = flat.shape
    out = pl.pallas_call(
        _cast_kernel,
        grid=(n // rows,),
        in_specs=[pl.BlockSpec((rows, c), lambda i: (i, 0))],
        out_specs=pl.BlockSpec((rows, c), lambda i: (i, 0)),
        out_shape=jax.ShapeDtypeStruct((n, c), BF16),
        compiler_params=_cparams(("arbitrary",)),
    )(flat)
    return out.reshape(w.shape)


def _adaln_kernel(c_ref, w_ref, b_ref, o_ref):
    c = c_ref[...]
    s = c * _sigmoid(c)
    s_hi, s_lo = _split_bf16(s)
    w_hi, w_lo = _split_bf16(w_ref[0])
    o_ref[0] = _dot(s_hi, w_hi) + _dot(s_lo, w_hi) + _dot(s_hi, w_lo) + b_ref[0]


def _adaln(c_all, w_ada, b_ada):
    nb = c_all.shape[0]
    tn = 1536
    return pl.pallas_call(
        _adaln_kernel,
        grid=(DEPTH, 6 * D_MODEL // tn),
        in_specs=[
            pl.BlockSpec((nb, D_MODEL), lambda l, j: (0, 0)),
            pl.BlockSpec((1, D_MODEL, tn), lambda l, j: (l, 0, j)),
            pl.BlockSpec((1, 1, tn), lambda l, j: (l, 0, j)),
        ],
        out_specs=pl.BlockSpec((1, nb, tn), lambda l, j: (l, 0, j)),
        out_shape=jax.ShapeDtypeStruct((DEPTH, nb, 6 * D_MODEL), F32),
        compiler_params=_cparams(("arbitrary", "arbitrary")),
    )(c_all, w_ada, b_ada.reshape(DEPTH, 1, 6 * D_MODEL))


def _in_proj_kernel(has_ff, *refs):
    if has_ff:
        (x_ref, ff_ref, g2_ref, sh_ref, sc_ref, nw_ref, w_ref, wif_ref, bif_ref,
         u_ref, kv_ref, if_ref, xo_ref, h_scr) = refs
    else:
        (x_ref, sh_ref, sc_ref, nw_ref, w_ref, wif_ref, bif_ref,
         u_ref, kv_ref, if_ref, h_scr) = refs
    j = pl.program_id(2)

    @pl.when(j == 0)
    def _():
        x = x_ref[0]
        if has_ff:
            x = x + g2_ref[0] * ff_ref[0]
            xo_ref[0] = x
        ms = jnp.mean(x * x, axis=-1, keepdims=True)
        h = x * lax.rsqrt(ms + EPS) * nw_ref[...]
        h = h * (1.0 + sc_ref[0]) + sh_ref[0]
        h16 = h.astype(BF16)
        h_scr[...] = h16
        if_ref[0] = _dot(h16, wif_ref[...]) + bif_ref[...]

    u = _dot(h_scr[...], w_ref[...])
    u_ref[0] = u.astype(BF16)

    @pl.when((j >= J_SK) & (j < J_SK + 2))
    def _():
        kv_ref[0] = u


def _in_proj(x, ff, g2, sh, sc, nw, w16, wif16, bif, tl):
    B, L, D = x.shape
    lm = sh.shape[1]
    has_ff = ff is not None
    tlm = 1 if lm == 1 else tl
    row = lambda b, i, j: (b, i, 0)
    mod = (lambda b, i, j: (b, 0, 0)) if lm == 1 else row
    fixed = lambda b, i, j: (0, 0)
    in_specs = [pl.BlockSpec((1, tl, D), row)]
    args = [x]
    if has_ff:
        in_specs += [pl.BlockSpec((1, tl, D), row), pl.BlockSpec((1, tlm, D), mod)]
        args += [ff, g2]
    in_specs += [
        pl.BlockSpec((1, tlm, D), mod),
        pl.BlockSpec((1, tlm, D), mod),
        pl.BlockSpec((1, D), fixed),
        pl.BlockSpec((D, TN_IN), lambda b, i, j: (0, j)),
        pl.BlockSpec((D, LANES), fixed),
        pl.BlockSpec((1, LANES), fixed),
    ]
    args += [sh, sc, nw, w16, wif16, bif]
    out_specs = [
        pl.BlockSpec((1, tl, TN_IN), lambda b, i, j: (b, i, j)),
        pl.BlockSpec((1, tl, W_SB), lambda b, i, j: (b, i, jnp.clip(j - J_SK, 0, 1))),
        pl.BlockSpec((1, tl, LANES), row),
    ]
    out_shape = [
        jax.ShapeDtypeStruct((B, L, U_W), BF16),
        jax.ShapeDtypeStruct((B, L, 2 * W_SB), F32),
        jax.ShapeDtypeStruct((B, L, LANES), F32),
    ]
    if has_ff:
        out_specs.append(pl.BlockSpec((1, tl, D), row))
        out_shape.append(jax.ShapeDtypeStruct((B, L, D), F32))
    return pl.pallas_call(
        functools.partial(_in_proj_kernel, has_ff),
        grid=(B, L // tl, U_W // TN_IN),
        in_specs=in_specs,
        out_specs=out_specs,
        out_shape=out_shape,
        scratch_shapes=[pltpu.VMEM((tl, D), BF16)],
        compiler_params=_cparams(("arbitrary", "arbitrary", "arbitrary")),
    )(*args)


def _mlstm_kernel(valid, q_ref, k_ref, v_ref, o_ref, if_ref, c0_ref, n0_ref, m0_ref, nw_ref,
                  h_ref, co_ref, no_ref, mo_ref, c_scr, n_scr, m_scr, gt_scr):
    hd = pl.program_id(1)
    c = pl.program_id(2)
    cs = M_CHUNK

    @pl.when(c == 0)
    def _():
        c_scr[...] = c0_ref[0, 0]
        n_scr[...] = n0_ref[0, 0]
        m_scr[...] = m0_ref[0, 0]

    q = q_ref[0]
    ksc = k_ref[0].astype(F32) * (DH_M ** -0.5)
    kb = ksc.astype(BF16)
    v = v_ref[0]
    gates = if_ref[0]
    gt_scr[...] = gates.T
    lane = lax.broadcasted_iota(jnp.int32, (cs, LANES), 1)
    i_col = jnp.sum(jnp.where(lane == hd, gates, 0.0), axis=1, keepdims=True)
    f_col = jnp.sum(jnp.where(lane == hd + NH_M, gates, 0.0), axis=1, keepdims=True)
    i_row = gt_scr[pl.ds(hd, 1), :]
    f_row = gt_scr[pl.ds(hd + NH_M, 1), :]
    lf_col = _log_sigmoid(f_col)
    lf_row = _log_sigmoid(f_row)
    t_idx = lax.broadcasted_iota(jnp.int32, (cs, cs), 0)
    s_idx = lax.broadcasted_iota(jnp.int32, (cs, cs), 1)
    if valid < cs:
        col_ok = t_idx[:, :1] < valid
        row_ok = s_idx[:1, :] < valid
        i_col = jnp.where(col_ok, i_col, NEG_BIG)
        lf_col = jnp.where(col_ok, lf_col, 0.0)
        i_row = jnp.where(row_ok, i_row, NEG_BIG)
        lf_row = jnp.where(row_ok, lf_row, 0.0)
    causal = s_idx <= t_idx
    F_col = jnp.sum(jnp.where(causal, lf_row, 0.0), axis=1, keepdims=True)
    F_row = jnp.sum(jnp.where(t_idx <= s_idx, lf_col, 0.0), axis=0, keepdims=True)
    F_end = jnp.sum(lf_row, axis=1, keepdims=True)
    m_prev = m_scr[:, 0:1]

    dlog = jnp.where(causal, F_col - F_row + i_row, -jnp.inf)
    b_inter = F_col + m_prev
    m_t = jnp.maximum(b_inter, jnp.max(dlog, axis=1, keepdims=True))
    s = _dot_nt(q, kb) * jnp.exp(dlog - m_t)
    a_inter = jnp.exp(b_inter - m_t)
    Cmat = c_scr[...]
    nvec = n_scr[...]
    num = _dot(s.astype(BF16), v) + a_inter * _dot_nt(q, Cmat.astype(BF16))
    qn = jnp.sum(q.astype(F32) * nvec, axis=1, keepdims=True)
    den = jnp.sum(s, axis=1, keepdims=True) + a_inter * qn
    den = jnp.maximum(jnp.abs(den), jnp.exp(-m_t))
    h = num / den

    dec_row = F_end - F_row + i_row
    dec_col = F_end - F_col + i_col
    m_new = jnp.maximum(F_end + m_prev, jnp.max(dec_row, axis=1, keepdims=True))
    w_col = jnp.exp(dec_col - m_new)
    keep = jnp.exp(F_end + m_prev - m_new)
    c_scr[...] = keep * Cmat + _dot_tn((w_col * v.astype(F32)).astype(BF16), kb)
    n_scr[...] = keep * nvec + jnp.sum(w_col * ksc, axis=0, keepdims=True)
    m_scr[...] = jnp.broadcast_to(m_new, m_scr.shape)

    mu = jnp.mean(h, axis=-1, keepdims=True)
    var = jnp.mean(jnp.square(h - mu), axis=-1, keepdims=True)
    hn = (h - mu) * lax.rsqrt(var + EPS) * nw_ref[0]
    h_ref[0] = (hn * _sigmoid(o_ref[0].astype(F32))).astype(BF16)

    @pl.when(c == pl.num_programs(2) - 1)
    def _():
        co_ref[0, 0] = c_scr[...]
        no_ref[0, 0] = n_scr[...]
        mo_ref[0, 0] = m_scr[...]


def _mlstm(um, if32, C0, n0, m0, mnorm_w, valid):
    B, L = um.shape[0], um.shape[1]
    nc = L // M_CHUNK
    blk = lambda off: pl.BlockSpec((1, M_CHUNK, DH_M), lambda b, h, c: (b, c, off + h))
    st = lambda r: pl.BlockSpec((1, 1, r, DH_M), lambda b, h, c: (b, h, 0, 0))
    return pl.pallas_call(
        functools.partial(_mlstm_kernel, valid),
        grid=(B, NH_M, nc),
        in_specs=[
            blk(0), blk(NH_M), blk(2 * NH_M), blk(3 * NH_M),
            pl.BlockSpec((1, M_CHUNK, LANES), lambda b, h, c: (b, c, 0)),
            st(DH_M), st(1), st(1),
            pl.BlockSpec((1, 1, DH_M), lambda b, h, c: (h, 0, 0)),
        ],
        out_specs=[
            pl.BlockSpec((1, M_CHUNK, DH_M), lambda b, h, c: (b, c, h)),
            st(DH_M), st(1), st(1),
        ],
        out_shape=[
            jax.ShapeDtypeStruct((B, L, W_M), BF16),
            jax.ShapeDtypeStruct((B, NH_M, DH_M, DH_M), F32),
            jax.ShapeDtypeStruct((B, NH_M, 1, DH_M), F32),
            jax.ShapeDtypeStruct((B, NH_M, 1, DH_M), F32),
        ],
        scratch_shapes=[
            pltpu.VMEM((DH_M, DH_M), F32),
            pltpu.VMEM((1, DH_M), F32),
            pltpu.VMEM((1, DH_M), F32),
            pltpu.VMEM((LANES, M_CHUNK), F32),
        ],
        compiler_params=_cparams(("arbitrary", "arbitrary", "arbitrary")),
    )(um, um, um, um, if32, C0, n0.reshape(B, NH_M, 1, DH_M),
      jnp.broadcast_to(m0[:, :, None, None], (B, NH_M, 1, DH_M)),
      mnorm_w.reshape(NH_M, 1, DH_M))


def _conv_kernel(tl, rb, a_ref, g_ref, st_ref, w_ref, b_ref, lw_ref, lb_ref,
                 o_ref, so_ref, buf, shifted):
    i = pl.program_id(1)
    sub = 8

    @pl.when(i == 0)
    def _():
        buf[0:CONV_HIST, :] = st_ref[0]

    a = a_ref[0].astype(F32) * _sigmoid(g_ref[0].astype(F32))
    buf[CONV_HIST:CONV_HIST + tl, :] = a
    pad = CONV_HIST - (CONV_W - 1)
    span = shifted.shape[1]
    for s in range(1, sub):
        shifted[s - 1] = buf[s:s + span, :]

    def block(r, carry):
        base = pl.multiple_of(r * rb, rb)
        acc = jnp.broadcast_to(b_ref[...], (rb, CONV_CH))
        for j in range(CONV_W):
            qq, s = divmod(j + pad, sub)
            start = pl.multiple_of(base + sub * qq, sub)
            if s == 0:
                rows = buf[pl.ds(start, rb), :]
            else:
                rows = shifted[s - 1, pl.ds(start, rb), :]
            acc = acc + w_ref[j:j + 1, :] * rows
        mu = jnp.mean(acc, axis=-1, keepdims=True)
        var = jnp.mean(jnp.square(acc - mu), axis=-1, keepdims=True)
        y = (acc - mu) * lax.rsqrt(var + EPS) * lw_ref[...] + lb_ref[...]
        o_ref[0, pl.ds(base, rb), :] = (y * _sigmoid(y)).astype(o_ref.dtype)
        return carry

    lax.fori_loop(0, tl // rb, block, 0)
    tail = buf[tl:tl + CONV_HIST, :]
    buf[0:CONV_HIST, :] = tail

    @pl.when(i == pl.num_programs(1) - 1)
    def _():
        so_ref[0] = tail


def _conv(ua, ug, off_a, off_g, state, dw_w, dw_b, ln_w, ln_b, tl):
    B, L = ua.shape[0], ua.shape[1]
    rb = min(tl, 32)
    pad = CONV_HIST - (CONV_W - 1)
    st = jnp.pad(state, ((0, 0), (pad, 0), (0, 0)))
    w = jnp.pad(dw_w, ((0, CONV_HIST - CONV_W), (0, 0)))
    vec = lambda b, i: (0, 0)
    out, so = pl.pallas_call(
        functools.partial(_conv_kernel, tl, rb),
        grid=(B, L // tl),
        in_specs=[
            pl.BlockSpec((1, tl, CONV_CH), lambda b, i: (b, i, off_a)),
            pl.BlockSpec((1, tl, CONV_CH), lambda b, i: (b, i, off_g)),
            pl.BlockSpec((1, CONV_HIST, CONV_CH), lambda b, i: (b, 0, 0)),
            pl.BlockSpec((CONV_HIST, CONV_CH), vec),
            pl.BlockSpec((1, CONV_CH), vec),
            pl.BlockSpec((1, CONV_CH), vec),
            pl.BlockSpec((1, CONV_CH), vec),
        ],
        out_specs=[
            pl.BlockSpec((1, tl, CONV_CH), lambda b, i: (b, i, 0)),
            pl.BlockSpec((1, CONV_HIST, CONV_CH), lambda b, i: (b, 0, 0)),
        ],
        out_shape=[
            jax.ShapeDtypeStruct((B, L, CONV_CH), ua.dtype),
            jax.ShapeDtypeStruct((B, CONV_HIST, CONV_CH), F32),
        ],
        scratch_shapes=[pltpu.VMEM((CONV_HIST + tl, CONV_CH), F32),
                        pltpu.VMEM((7, CONV_HIST + tl - 8, CONV_CH), F32)],
        compiler_params=_cparams(("arbitrary", "arbitrary")),
    )(ua, ug, st, w, dw_b.reshape(1, -1), ln_w.reshape(1, -1), ln_b.reshape(1, -1))
    return out, so[:, pad:, :]


def _sb_tri(tk):
    r = lax.broadcasted_iota(jnp.int32, (tk, tk + LANES), 0)
    c = lax.broadcasted_iota(jnp.int32, (tk, tk + LANES), 1)
    return jnp.where((c >= tk) | (r > c), 1.0, 0.0).astype(BF16)


SB_HEADS = 4
SB_W = SB_HEADS * DH_SB
SB_TQ = 256
SB_CB = 256


def _sb_prompt_kernel(tq, tk, bias_ref, tri_ref, q_ref, k_ref, v_ref, o_ref, acc_scr, car_scr):
    p = pl.program_id(1)
    qi = pl.program_id(2)
    cb = min(SB_CB, tk)
    nb = tk // cb
    nsub = tq // LANES
    ng = nsub * SB_HEADS
    lane = lax.broadcasted_iota(jnp.int32, (1, SB_W), 1)
    q2 = q_ref[0].astype(F32) * (DH_SB ** -0.5)
    qs = jnp.concatenate(
        [jnp.where((lane // DH_SB) == hh, q2[r * LANES:(r + 1) * LANES], 0.0).astype(BF16)
         for r in range(nsub) for hh in range(SB_HEADS)], axis=0)
    bias = [bias_ref[SB_HEADS * p + hh] for hh in range(SB_HEADS)]
    acc_scr[...] = jnp.zeros_like(acc_scr)
    car_scr[...] = jnp.zeros_like(car_scr)
    rr = lax.broadcasted_iota(jnp.int32, (LANES, tk), 0)
    cc = lax.broadcasted_iota(jnp.int32, (LANES, tk), 1)
    rows = lambda a, i: a[i * LANES:(i + 1) * LANES]

    def tile(kt, masked):
        start = pl.multiple_of(kt * tk, tk)
        ktile = k_ref[0, pl.ds(start, tk), :]
        vtile = v_ref[0, pl.ds(start, tk), :]
        if masked:
            visible = [(cc + kt * tk) < (rr + (qi * tq + r * LANES)) for r in range(nsub)]
        z_all = _dot_nt(qs, ktile)
        log_beta, log_1m = [], []
        for g in range(ng):
            z = rows(z_all, g) + bias[g % SB_HEADS]
            sp = jnp.log(1.0 + jnp.exp(-jnp.abs(z)))
            lb = jnp.minimum(z, 0.0) - sp
            l1 = lb - z
            if masked:
                l1 = jnp.where(visible[g // SB_HEADS], l1, 0.0)
            log_beta.append(lb)
            log_1m.append(l1)
        stack = jnp.concatenate([log_1m[g][:, j * cb:(j + 1) * cb].astype(BF16)
                                 for j in range(nb) for g in range(ng)], axis=0)
        later_all = _dot(stack, tri_ref[...])
        a_rows = []
        for g in range(ng):
            car = car_scr[g]
            parts = [None] * nb
            for j in reversed(range(nb)):
                sl = slice(j * cb, (j + 1) * cb)
                later = rows(later_all, j * ng + g)
                parts[j] = jnp.exp(log_beta[g][:, sl] + later + jnp.tile(car, (1, cb // LANES)))
                car = car + jnp.sum(log_1m[g][:, sl], axis=1, keepdims=True)
            car_scr[g] = car
            A = jnp.concatenate(parts, axis=1)
            if masked:
                A = jnp.where(visible[g // SB_HEADS], A, 0.0)
            a_rows.append(A.astype(BF16))
        acc_scr[...] = acc_scr[...] + _dot(jnp.concatenate(a_rows, axis=0), vtile)

    kt_d = (qi * tq) // tk
    tile(kt_d, True)

    def body(it, carry):
        tile(kt_d - 1 - it, False)
        return carry

    lax.fori_loop(0, kt_d, body, 0)
    head = lax.broadcasted_iota(jnp.int32, (LANES, SB_W), 1) // DH_SB
    for r in range(nsub):
        out = acc_scr[(r * SB_HEADS) * LANES:(r * SB_HEADS + 1) * LANES]
        for hh in range(1, SB_HEADS):
            g = r * SB_HEADS + hh
            out = jnp.where(head == hh, acc_scr[g * LANES:(g + 1) * LANES], out)
        o_ref[0, r * LANES:(r + 1) * LANES, :] = out.astype(BF16)


def _sb_prompt(u16, sb_b):
    B, L = u16.shape[0], u16.shape[1]
    tq = min(SB_TQ, L)
    tk = min(512, L)
    cb = min(SB_CB, tk)
    c0 = U_S // SB_W
    nq = W_SB // SB_W
    r = lax.broadcasted_iota(jnp.int32, (cb, cb), 0)
    c = lax.broadcasted_iota(jnp.int32, (cb, cb), 1)
    tri = jnp.where(r > c, 1.0, 0.0).astype(BF16)
    ng = (tq // LANES) * SB_HEADS
    return pl.pallas_call(
        functools.partial(_sb_prompt_kernel, tq, tk),
        grid=(B, nq, L // tq),
        in_specs=[
            pl.BlockSpec(memory_space=pltpu.SMEM),
            pl.BlockSpec((cb, cb), lambda b, p, i: (0, 0)),
            pl.BlockSpec((1, tq, SB_W), lambda b, p, i: (b, i, c0 + p)),
            pl.BlockSpec((1, L, SB_W), lambda b, p, i: (b, 0, c0 + nq + p)),
            pl.BlockSpec((1, L, SB_W), lambda b, p, i: (b, 0, c0 + 2 * nq + p)),
        ],
        out_specs=pl.BlockSpec((1, tq, SB_W), lambda b, p, i: (b, i, p)),
        out_shape=jax.ShapeDtypeStruct((B, L, W_SB), BF16),
        scratch_shapes=[pltpu.VMEM((ng * LANES, SB_W), F32),
                        pltpu.VMEM((ng, LANES, LANES), F32)],
        compiler_params=_cparams(("arbitrary", "arbitrary", "arbitrary")),
    )(sb_b, tri, u16, u16, u16)


PAGES_PER_STEP = 8


def _sb_sample_kernel(lq, pt_ref, qbd_ref, bias_ref, tri_ref, kn_ref, vn_ref, *refs):
    g = PAGES_PER_STEP
    k_refs = refs[:g]
    v_refs = refs[g:2 * g]
    o_ref = refs[2 * g]
    acc_scr, car_scr = refs[2 * g + 1:]
    s = pl.program_id(1)
    nr = NH_SB * lq
    qbd = qbd_ref[0]
    bias = bias_ref[...]
    tri = tri_ref[...]

    def scores(k16, visible):
        z = _dot_nt(qbd, k16) + jnp.tile(bias, (1, k16.shape[0] // LANES))
        sp = jnp.log(1.0 + jnp.exp(-jnp.abs(z)))
        log_beta = jnp.minimum(z, 0.0) - sp
        log_1m = log_beta - z
        if visible is not None:
            log_1m = jnp.where(visible, log_1m, 0.0)
        return log_beta, log_1m

    def weights(log_beta, log_1m, car):
        hi, lo = _split_bf16(log_1m)
        parts = []
        for j in range(log_1m.shape[1] // LANES):
            sl = slice(j * LANES, (j + 1) * LANES)
            cum = _dot(hi[:, sl], tri) + _dot(lo[:, sl], tri)
            parts.append(jnp.exp(log_beta[:, sl] + cum[:, :LANES] + car))
            car = car + cum[:, LANES:]
        return parts, car

    @pl.when(s == 0)
    def _():
        r = lax.broadcasted_iota(jnp.int32, (nr, PAGE_SIZE), 0)
        c = lax.broadcasted_iota(jnp.int32, (nr, PAGE_SIZE), 1)
        visible = c < (r % lq)
        log_beta, log_1m = scores(kn_ref[0].astype(BF16), visible)
        parts, car = weights(log_beta, log_1m, jnp.zeros((nr, LANES), F32))
        A = jnp.where(visible, parts[0], 0.0)
        acc_scr[...] = _dot(A.astype(BF16), vn_ref[0].astype(BF16))
        car_scr[...] = car

    k16 = jnp.concatenate([k_refs[gi][0, 0] for gi in range(g)], axis=0)
    v16 = jnp.concatenate([v_refs[gi][0, 0] for gi in range(g)], axis=0)
    log_beta, log_1m = scores(k16, None)
    parts, car = weights(log_beta, log_1m, car_scr[...])
    A = jnp.concatenate(parts, axis=1).astype(BF16)
    acc = acc_scr[...] + _dot(A, v16)
    acc_scr[...] = acc
    car_scr[...] = car

    @pl.when(s == pl.num_programs(1) - 1)
    def _():
        r = lax.broadcasted_iota(jnp.int32, (nr, W_SB), 0)
        c = lax.broadcasted_iota(jnp.int32, (nr, W_SB), 1)
        m = jnp.where((r // lq) == (c // DH_SB), acc, 0.0)
        out = m[0:lq]
        for hh in range(1, NH_SB):
            out = out + m[hh * lq:(hh + 1) * lq]
        o_ref[0] = out


def _sb_sample(q16, k_new, v_new, cache_k, cache_v, layer, page_table, sb_b):
    B, lq, _ = q16.shape
    n_pages = page_table.shape[1]
    g = PAGES_PER_STEP
    nr = NH_SB * lq
    k_new = jnp.pad(k_new, ((0, 0), (0, PAGE_SIZE - lq), (0, 0)))
    v_new = jnp.pad(v_new, ((0, 0), (0, PAGE_SIZE - lq), (0, 0)))
    qh = (q16.astype(F32) * (DH_SB ** -0.5)).reshape(B, lq, NH_SB, DH_SB)
    qbd = jnp.einsum('bihd,hg->bhigd', qh, jnp.eye(NH_SB, dtype=F32)).reshape(B, nr, W_SB).astype(BF16)
    bias = jnp.broadcast_to(jnp.repeat(sb_b.astype(F32), lq)[:, None], (nr, LANES))

    def page_spec(gi):
        return pl.BlockSpec((1, 1, PAGE_SIZE, W_SB),
                            lambda b, s, pt: (layer, pt[b, n_pages - 1 - (s * g + gi)], 0, 0))

    fixed = lambda b, s, pt: (0, 0)
    seq = lambda b, s, pt: (b, 0, 0)
    grid_spec = pltpu.PrefetchScalarGridSpec(
        num_scalar_prefetch=1,
        grid=(B, n_pages // g),
        in_specs=[
            pl.BlockSpec((1, nr, W_SB), seq),
            pl.BlockSpec((nr, LANES), fixed),
            pl.BlockSpec((PAGE_SIZE, PAGE_SIZE + LANES), fixed),
            pl.BlockSpec((1, PAGE_SIZE, W_SB), seq),
            pl.BlockSpec((1, PAGE_SIZE, W_SB), seq),
        ] + [page_spec(gi) for gi in range(g)] * 2,
        out_specs=pl.BlockSpec((1, lq, W_SB), seq),
        scratch_shapes=[pltpu.VMEM((nr, W_SB), F32), pltpu.VMEM((nr, LANES), F32)],
    )
    return pl.pallas_call(
        functools.partial(_sb_sample_kernel, lq),
        grid_spec=grid_spec,
        out_shape=jax.ShapeDtypeStruct((B, lq, W_SB), F32),
        compiler_params=_cparams(("arbitrary", "arbitrary")),
    )(page_table, qbd, bias, _sb_tri(PAGE_SIZE), k_new, v_new, *([cache_k] * g), *([cache_v] * g))


def _merge_kernel(hm_ref, hc_ref, ha_ref, gm_ref, gc_ref, ga_ref, x_ref, g1_ref, sh_ref, sc_ref,
                  wpm_ref, wpc_ref, wpa_ref, wo_ref, nw_ref, rwh_ref, rwl_ref, rb_ref,
                  xo_ref, h2_ref, gate_ref):
    y = _sigmoid(gm_ref[0].astype(F32)) * _dot(hm_ref[0], wpm_ref[...])
    y = y + _sigmoid(gc_ref[0].astype(F32)) * _dot(hc_ref[0], wpc_ref[...])
    y = y + _sigmoid(ga_ref[0].astype(F32)) * _dot(ha_ref[0], wpa_ref[...])
    mix = _dot(y.astype(BF16), wo_ref[...])
    x = x_ref[0] + g1_ref[0] * mix
    xo_ref[0] = x
    ms = jnp.mean(x * x, axis=-1, keepdims=True)
    h = x * lax.rsqrt(ms + EPS) * nw_ref[...]
    h = h * (1.0 + sc_ref[0]) + sh_ref[0]
    h_hi, h_lo = _split_bf16(h)
    h2_ref[0] = h_hi
    logits = (_dot(h_hi, rwh_ref[...]) + _dot(h_lo, rwh_ref[...]) + _dot(h_hi, rwl_ref[...])
              + rb_ref[...])
    lane = lax.broadcasted_iota(jnp.int32, logits.shape, 1).astype(F32)
    work = logits
    sel = jnp.zeros(logits.shape, F32)
    vmax = None
    for _ in range(TOP_K):
        mx = jnp.max(work, axis=-1, keepdims=True)
        if vmax is None:
            vmax = mx
        first = jnp.min(jnp.where(work == mx, lane, float(LANES)), axis=-1, keepdims=True)
        pick = lane == first
        sel = jnp.where(pick, 1.0, sel)
        work = jnp.where(pick, -jnp.inf, work)
    e = sel * jnp.exp(logits - vmax)
    gate_ref[0] = e / jnp.sum(e, axis=-1, keepdims=True)


def _merge(hm, hc, ha, u16, x, g1, sh2, sc2, wpm, wpc, wpa, wo, nw, rw_hi, rw_lo, rb, tl):
    B, L, D = x.shape
    lm = g1.shape[1]
    tlm = 1 if lm == 1 else tl
    row = lambda b, i: (b, i, 0)
    mod = (lambda b, i: (b, 0, 0)) if lm == 1 else row
    fixed = lambda b, i: (0, 0)
    gb = U_G // D
    half = lambda: pl.BlockSpec((1, tl, W_M), row)
    gate = lambda n: pl.BlockSpec((1, tl, D), lambda b, i: (b, i, gb + n))
    modspec = lambda: pl.BlockSpec((1, tlm, D), mod)
    return pl.pallas_call(
        _merge_kernel,
        grid=(B, L // tl),
        in_specs=[
            half(), half(), half(), gate(0), gate(1), gate(2),
            pl.BlockSpec((1, tl, D), row), modspec(), modspec(), modspec(),
            pl.BlockSpec((W_M, D), fixed), pl.BlockSpec((CONV_CH, D), fixed),
            pl.BlockSpec((W_SB, D), fixed), pl.BlockSpec((D, D), fixed),
            pl.BlockSpec((1, D), fixed),
            pl.BlockSpec((D, LANES), fixed), pl.BlockSpec((D, LANES), fixed),
            pl.BlockSpec((1, LANES), fixed),
        ],
        out_specs=[
            pl.BlockSpec((1, tl, D), row),
            pl.BlockSpec((1, tl, D), row),
            pl.BlockSpec((1, tl, LANES), row),
        ],
        out_shape=[
            jax.ShapeDtypeStruct((B, L, D), F32),
            jax.ShapeDtypeStruct((B, L, D), BF16),
            jax.ShapeDtypeStruct((B, L, LANES), F32),
        ],
        compiler_params=_cparams(("arbitrary", "arbitrary")),
    )(hm, hc, ha, u16, u16, u16, x, g1, sh2, sc2, wpm, wpc, wpa, wo, nw, rw_hi, rw_lo, rb)


MOE_ROWS = 192


def _moe_kernel(cnt_ref, h_ref, gt_ref, u_ref, w1_ref, b1_ref, w2_ref, b2_ref, o_ref, rank_scr):
    i = pl.program_id(0)
    e = pl.program_id(1)
    tm = h_ref.shape[0]

    @pl.when(e == 0)
    def _():
        o_ref[...] = jnp.zeros_like(o_ref)
        routed = gt_ref[...] > 0.0
        before = _dot(jnp.where(routed, 1.0, 0.0).astype(BF16), u_ref[...])
        rank_scr[...] = jnp.where(routed, before, -1.0)

    rank = rank_scr[pl.ds(e, 1), :]
    gate = gt_ref[pl.ds(e, 1), :]
    slot = lax.broadcasted_iota(jnp.int32, (MOE_ROWS, tm), 0).astype(F32)

    def chunk(c, carry):
        sel = jnp.where((rank - (c * MOE_ROWS).astype(F32)) == slot, 1.0, 0.0)
        sel16 = sel.astype(BF16)
        g_row = jnp.sum(sel * gate, axis=1, keepdims=True)
        x = _dot(sel16, h_ref[...]).astype(BF16)
        z = _dot(x, w1_ref[0]) + b1_ref[0]
        z_glu = jnp.minimum(z[:, :D_FF], SWIGLU_LIMIT)
        z_lin = jnp.clip(z[:, D_FF:], -SWIGLU_LIMIT, SWIGLU_LIMIT)
        act = z_glu * _sigmoid(SWIGLU_ALPHA * z_glu) * (z_lin + 1.0)
        y = _dot(act.astype(BF16), w2_ref[0]) + b2_ref[0]
        o_ref[...] += _dot_tn(sel16, (g_row * y).astype(BF16))
        return carry

    n_routed = cnt_ref[i, e]
    lax.fori_loop(0, (n_routed + MOE_ROWS - 1) // MOE_ROWS, chunk, 0)


def _moe(h2, gates, w1, b1, w2, b2, tm):
    N, D = h2.shape
    gt = gates.T
    cnt = jnp.sum((gates > 0.0).reshape(N // tm, tm, LANES), axis=1).astype(jnp.int32)
    r = lax.broadcasted_iota(jnp.int32, (tm, tm), 0)
    c = lax.broadcasted_iota(jnp.int32, (tm, tm), 1)
    before = jnp.where(r < c, 1.0, 0.0).astype(BF16)
    grid_spec = pltpu.PrefetchScalarGridSpec(
        num_scalar_prefetch=1,
        grid=(N // tm, N_EXP),
        in_specs=[
            pl.BlockSpec((tm, D), lambda i, e, cnt: (i, 0)),
            pl.BlockSpec((LANES, tm), lambda i, e, cnt: (0, i)),
            pl.BlockSpec((tm, tm), lambda i, e, cnt: (0, 0)),
            pl.BlockSpec((1, D, 2 * D_FF), lambda i, e, cnt: (e, 0, 0)),
            pl.BlockSpec((1, 1, 2 * D_FF), lambda i, e, cnt: (e, 0, 0)),
            pl.BlockSpec((1, D_FF, D), lambda i, e, cnt: (e, 0, 0)),
            pl.BlockSpec((1, 1, D), lambda i, e, cnt: (e, 0, 0)),
        ],
        out_specs=pl.BlockSpec((tm, D), lambda i, e, cnt: (i, 0)),
        scratch_shapes=[pltpu.VMEM((LANES, tm), F32)],
    )
    return pl.pallas_call(
        _moe_kernel,
        grid_spec=grid_spec,
        out_shape=jax.ShapeDtypeStruct((N, D), F32),
        compiler_params=_cparams(("arbitrary", "arbitrary")),
    )(cnt, h2, gt, before, w1, b1.reshape(N_EXP, 1, 2 * D_FF), w2, b2.reshape(N_EXP, 1, D))


def _final_kernel(x_ref, ff_ref, g2_ref, w_ref, o_ref):
    x = x_ref[0] + g2_ref[0] * ff_ref[0]
    ms = jnp.mean(x * x, axis=-1, keepdims=True)
    o_ref[0] = x * lax.rsqrt(ms + EPS) * w_ref[...]


def _final(x, ff, g2, w, tl):
    B, L, D = x.shape
    lm = g2.shape[1]
    tlm = 1 if lm == 1 else tl
    row = lambda b, i: (b, i, 0)
    mod = (lambda b, i: (b, 0, 0)) if lm == 1 else row
    return pl.pallas_call(
        _final_kernel,
        grid=(B, L // tl),
        in_specs=[
            pl.BlockSpec((1, tl, D), row), pl.BlockSpec((1, tl, D), row),
            pl.BlockSpec((1, tlm, D), mod), pl.BlockSpec((1, D), lambda b, i: (0, 0)),
        ],
        out_specs=pl.BlockSpec((1, tl, D), row),
        out_shape=jax.ShapeDtypeStruct((B, L, D), F32),
        compiler_params=_cparams(("arbitrary", "arbitrary")),
    )(x, ff, g2, w.reshape(1, D))


def _prep_layer_weights(l, w_in, mlstm_gate_b, router_w, router_b):
    wl = w_in[l]
    w_main = jnp.concatenate(
        [wl[:, OFF_MQ:OFF_MIF], wl[:, OFF_CG:OFF_SQ], wl[:, OFF_G:], wl[:, OFF_SQ:OFF_G]], axis=1)
    wif = jnp.pad(wl[:, OFF_MIF:OFF_CG], ((0, 0), (0, LANES - 2 * NH_M)))
    bif = jnp.pad(mlstm_gate_b[l], (0, LANES - 2 * NH_M)).reshape(1, LANES)
    rw = jnp.pad(router_w[l], ((0, 0), (0, LANES - N_EXP)))
    rw_hi = rw.astype(BF16)
    rw_lo = (rw - rw_hi.astype(F32)).astype(BF16)
    rb = jnp.pad(router_b[l], (0, LANES - N_EXP), constant_values=NEG_BIG).reshape(1, LANES)
    return w_main.astype(BF16), wif.astype(BF16), bif, rw_hi, rw_lo, rb


def _trunk(x, mods, C0, n0, m0, conv0, paged, W, tl, tl_mg, tm):
    B, L, D = x.shape
    sh1, sc1, g1, sh2, sc2, g2 = mods
    ks, vs, Cs, ns, ms, bufs = [], [], [], [], [], []
    ff = None
    for l in range(DEPTH):
        w16, wif16, bif, rw_hi, rw_lo, rb = W['prep'][l]
        nw1 = W['norm1_w'][l].reshape(1, D)
        if ff is None:
            u16, kv32, if32 = _in_proj(x, None, None, sh1[l], sc1[l], nw1, w16, wif16, bif, tl)
        else:
            u16, kv32, if32, x = _in_proj(x, ff, g2[l - 1], sh1[l], sc1[l], nw1, w16, wif16, bif, tl)
        if paged is None:
            hm, Cn, nn, mn = _mlstm(u16, if32, C0[l], n0[l], m0[l], W['mlstm_norm_w'][l], M_CHUNK)
            hc, buf = _conv(u16, u16, U_CG // CONV_CH, U_CG // CONV_CH + 1, conv0[l],
                            W['conv_dw_w'][l], W['conv_dw_b'][l], W['conv_ln_w'][l], W['conv_ln_b'][l],
                            min(512, L))
            ha = _sb_prompt(u16, W['sb_bias'][l])
            k_new = kv32[..., :W_SB].reshape(B, L, NH_SB, DH_SB)
            v_new = kv32[..., W_SB:].reshape(B, L, NH_SB, DH_SB)
        else:
            cache_k, cache_v, page_table, bs, ls = paged
            us = u16.reshape(bs, ls, U_W)
            padl = M_CHUNK - ls
            um = jnp.pad(us[..., :4 * W_M], ((0, 0), (0, padl), (0, 0)))
            ifp = jnp.pad(if32.reshape(bs, ls, LANES), ((0, 0), (0, padl), (0, 0)))
            hm, Cn, nn, mn = _mlstm(um, ifp, C0[l], n0[l], m0[l], W['mlstm_norm_w'][l], ls)
            hm = hm[:, :ls].reshape(B, L, W_M)
            ucg = us[..., U_CG:U_CG + 2 * CONV_CH].astype(F32)
            hc, buf = _conv(ucg, ucg, 0, 1, conv0[l],
                            W['conv_dw_w'][l], W['conv_dw_b'][l], W['conv_ln_w'][l], W['conv_ln_b'][l], ls)
            hc = hc.reshape(B, L, CONV_CH).astype(BF16)
            kvs = kv32.reshape(bs, ls, 2 * W_SB)
            ha = _sb_sample(us[..., U_S:U_S + W_SB], kvs[..., :W_SB], kvs[..., W_SB:],
                            cache_k, cache_v, l,
                            page_table, W['sb_bias'][l]).reshape(B, L, W_SB).astype(BF16)
            k_new = kvs[..., :W_SB].reshape(bs, ls, NH_SB, DH_SB)
            v_new = kvs[..., W_SB:].reshape(bs, ls, NH_SB, DH_SB)
        x, h2, gates = _merge(hm, hc, ha, u16, x, g1[l], sh2[l], sc2[l],
                              W['w_pm16'][l], W['w_pc16'][l], W['w_pa16'][l], W['w_out16'][l],
                              W['norm2_w'][l].reshape(1, D), rw_hi, rw_lo, rb, tl_mg)
        ff = _moe(h2.reshape(B * L, D), gates.reshape(B * L, LANES),
                  W['exp_w1_16'][l], W['exp_b1'][l], W['exp_w2_16'][l], W['exp_b2'][l], tm).reshape(B, L, D)
        ks.append(k_new)
        vs.append(v_new)
        Cs.append(Cn)
        ns.append(nn[:, :, 0, :])
        ms.append(mn[:, :, 0, 0])
        bufs.append(buf)
    y = _final(x, ff, g2[DEPTH - 1], W['final_w'], tl)
    return y, jnp.stack(ks), jnp.stack(vs), jnp.stack(Cs), jnp.stack(ns), jnp.stack(ms), jnp.stack(bufs)


def kernel(x_prompt, x_sample, c_prompt, c_sample, cache_k, cache_v, state_C, state_n, state_m, state_conv, page_table, w_ada, b_ada, norm1_w, norm2_w, w_in, mlstm_gate_b, mlstm_norm_w, w_pm, conv_dw_w, conv_dw_b, conv_ln_w, conv_ln_b, w_pc, w_pa, sb_bias, w_out, router_w, router_b, exp_w1, exp_b1, exp_w2, exp_b2, final_w):
    BP, LP, D = x_prompt.shape
    BS, LS, _ = x_sample.shape
    W = dict(norm1_w=norm1_w, norm2_w=norm2_w, mlstm_norm_w=mlstm_norm_w,
             conv_dw_w=conv_dw_w, conv_dw_b=conv_dw_b, conv_ln_w=conv_ln_w, conv_ln_b=conv_ln_b,
             sb_bias=sb_bias, exp_b1=exp_b1, exp_b2=exp_b2, final_w=final_w,
             w_pm16=w_pm.astype(BF16), w_pc16=w_pc.astype(BF16), w_pa16=w_pa.astype(BF16),
             w_out16=w_out.astype(BF16),
             exp_w1_16=_cast_bf16(exp_w1, 1024), exp_w2_16=_cast_bf16(exp_w2, 2048))
    W['prep'] = [_prep_layer_weights(l, w_in, mlstm_gate_b, router_w, router_b) for l in range(DEPTH)]

    nb = BP + BS
    nbp = -(-nb // 8) * 8
    c_all = jnp.pad(jnp.concatenate([c_prompt, c_sample], axis=0), ((0, nbp - nb), (0, 0)))
    mod = _adaln(c_all, w_ada, b_ada)
    mods_p = [mod[:, :BP, None, i * D:(i + 1) * D] for i in range(6)]
    mods_s = [jnp.repeat(mod[:, BP:nb, i * D:(i + 1) * D], LS, axis=1)[:, None] for i in range(6)]

    ck16 = cache_k.reshape(DEPTH, -1, PAGE_SIZE, W_SB).astype(BF16)
    cv16 = cache_v.reshape(DEPTH, -1, PAGE_SIZE, W_SB).astype(BF16)
    zC = jnp.zeros((DEPTH, BP, NH_M, DH_M, DH_M), F32)
    zn = jnp.zeros((DEPTH, BP, NH_M, DH_M), F32)
    zm = jnp.zeros((DEPTH, BP, NH_M), F32)
    zconv = jnp.zeros((DEPTH, BP, CONV_W - 1, CONV_CH), F32)
    yp, *outs_p = _trunk(x_prompt, mods_p, zC, zn, zm, zconv, None, W,
                         min(1024, LP), min(512, LP), min(1024, BP * LP))
    ys, *outs_s = _trunk(x_sample.reshape(1, BS * LS, D), mods_s, state_C, state_n, state_m, state_conv,
                         (ck16, cv16, page_table, BS, LS), W, BS * LS, BS * LS, BS * LS)
    return (yp, ys.reshape(BS, LS, D)) + tuple(outs_p) + tuple(outs_s)
```
